```python
import math
import jax
import jax.numpy as jnp
from jax import lax
import numpy as np

D_MODEL = 1024
BATCH = 32
SEQ = 256
DEPTH = 2
DEC_BATCH = 8
DEC_SEQ = 2048
PAST_LEN = 512

GRID_W = 64
N_MIXERS = 2
N_HYENA_LAYERS = (DEPTH + 1) // 2
N_ATTN_LAYERS = DEPTH // 2
SHORT_CONV = 3
FILTER_HIDDEN = 64
N_BANDS = 16
POS_EMB_DIM = 1 + 2 * N_BANDS
DECAY_TARGET = 1e-2
FAST_DECAY_PCT = 0.3
SLOW_DECAY_PCT = 1.5
N_HEADS = 16
HEAD_DIM = D_MODEL // N_HEADS
WIN_ROWS = 8
WIN_COLS = 16
Q_BLOCK = 128
N_GROUPS = 4
EXPERTS_PER_GROUP = 4
N_EXPERTS = N_GROUPS * EXPERTS_PER_GROUP
TOP_K = 2
D_EXPERT = 512
DEEPNORM_ALPHA = (2 * DEPTH) ** 0.25
DEEPNORM_BETA = (8 * DEPTH) ** -0.25
LN_EPS = 1e-5

kernel_name = 'hybrid_hyena_natten_hmoe_diffusion_step'


def _layer_norm(x, g, b):
    xf = x.astype(jnp.float32)
    mu = jnp.mean(xf, axis=-1, keepdims=True)
    var = jnp.mean(jnp.square(xf - mu), axis=-1, keepdims=True)
    y = (xf - mu) * lax.rsqrt(var + LN_EPS) * g.astype(jnp.float32) + b.astype(jnp.float32)
    return y.astype(x.dtype)


def _adaln(cond, w, b):
    mod = jax.nn.silu(cond) @ w + b
    return [m[:, None, :] for m in jnp.split(mod, 6, axis=-1)]


def _modulate(x, shift, scale):
    return x * (1 + scale) + shift


def _short_conv(x, w, b):
    L = x.shape[1]
    pad = SHORT_CONV // 2
    xp = jnp.pad(x, ((0, 0), (pad, SHORT_CONV - 1 - pad), (0, 0)))
    y = b
    for j in range(SHORT_CONV):
        y = y + xp[:, j:j + L] * w[j]
    return y


def _hyena_filter(L, f_w1, f_b1, f_w2, f_b2, f_w3, f_freq, decay):
    pos = jnp.arange(L, dtype=jnp.float32)[:, None]
    t = pos / max(L - 1, 1)
    bands = jnp.linspace(1e-4, N_BANDS - 1, N_BANDS, dtype=jnp.float32)
    ang = (2.0 * math.pi / L) * pos * bands
    feat = jnp.concatenate([t, jnp.cos(ang), -jnp.sin(ang)], axis=-1)
    h = jnp.sin(f_freq * (feat @ f_w1 + f_b1))
    h = jnp.sin(f_freq * (h @ f_w2 + f_b2))
    h = (h @ f_w3).astype(jnp.float32) * jnp.exp(-t * jnp.abs(decay).astype(jnp.float32))
    return h[:, :D_MODEL], h[:, D_MODEL:]


def _bidir_long_conv(u, h_fwd, h_bwd):
    L = u.shape[1]
    n = 2 * L
    k = jnp.concatenate([h_fwd, jnp.zeros((1, D_MODEL), jnp.float32), h_bwd[1:][::-1]], axis=0)
    k_f = jnp.fft.rfft(k, n=n, axis=0)
    u_f = jnp.fft.rfft(u.astype(jnp.float32), n=n, axis=1)
    return jnp.fft.irfft(u_f * k_f[None], n=n, axis=1)[:, :L]


def _hyena(x, w_in, b_in, conv_w, conv_b, f_w1, f_b1, f_w2, f_b2, f_w3, f_freq, decay, d_skip, w_out, b_out):
    L = x.shape[1]
    z = _short_conv(x @ w_in + b_in, conv_w, conv_b)
    x0, x1, v = jnp.split(z, 3, axis=-1)
    h_fwd, h_bwd = _hyena_filter(L, f_w1, f_b1, f_w2, f_b2, f_w3, f_freq, decay)
    u = (v * x1).astype(jnp.float32)
    y = (_bidir_long_conv(u, h_fwd, h_bwd) + u * d_skip.astype(jnp.float32)) * x0.astype(jnp.float32)
    return y.astype(x.dtype) @ w_out + b_out


def _qkv(x, w, b):
    B, L, _ = x.shape
    q, k, v = jnp.split(x @ w + b, 3, axis=-1)
    split = lambda t: t.reshape(B, L, N_HEADS, HEAD_DIM).transpose(0, 2, 1, 3)
    return split(q), split(k), split(v)


def _merge_heads(t):
    B, H, L, Dh = t.shape
    return t.transpose(0, 2, 1, 3).reshape(B, L, H * Dh)


def _context_attention(q, k, v):
    B, H, L, Dh = q.shape
    nb = L // Q_BLOCK
    scale = HEAD_DIM ** -0.5
    qb = q.reshape(B, H, nb, Q_BLOCK, Dh).transpose(2, 0, 1, 3, 4)

    def block(qi):
        s = jnp.einsum('bhqd,bhkd->bhqk', qi, k).astype(jnp.float32) * scale
        p = jax.nn.softmax(s, axis=-1).astype(v.dtype)
        return jnp.einsum('bhqk,bhkd->bhqd', p, v)

    o = lax.map(block, qb)
    return o.transpose(1, 2, 0, 3, 4).reshape(B, H, L, Dh)


def _neighbourhood_attention(q, k, v, k_ctx, v_ctx, rpb):
    B, H, L, Dh = q.shape
    rows = L // GRID_W
    kh = min(WIN_ROWS, rows)
    kw = WIN_COLS
    scale = HEAD_DIM ** -0.5
    r = jnp.arange(rows)
    row_start = jnp.clip(r - kh // 2, 0, rows - kh)
    row_idx = row_start[:, None] + jnp.arange(kh)[None, :]
    col = jnp.arange(GRID_W)
    col_start = jnp.clip(col - kw // 2, 0, GRID_W - kw)
    in_band = (col[None, :] >= col_start[:, None]) & (col[None, :] < col_start[:, None] + kw)
    dr_idx = row_idx - r[:, None] + (WIN_ROWS - 1)
    dc_idx = jnp.clip(col[None, :] - col[:, None], -(kw - 1), kw - 1) + (WIN_COLS - 1)
    bias = rpb[:, dr_idx[:, None, :, None], dc_idx[None, :, None, :]].astype(jnp.float32)
    bias = jnp.where(in_band[None, None, :, None, :], bias, -jnp.inf)

    def one_request(args):
        qi, ki, vi, kci, vci = args
        qg = qi.reshape(H, rows, GRID_W, Dh)
        kr = ki.reshape(H, rows, GRID_W, Dh)[:, row_idx]
        vr = vi.reshape(H, rows, GRID_W, Dh)[:, row_idx]
        s_loc = jnp.einsum('hrqd,hrkwd->hrqkw', qg, kr).astype(jnp.float32) * scale + bias
        s_ctx = jnp.einsum('hrqd,hcd->hrqc', qg, kci).astype(jnp.float32) * scale
        m = jnp.maximum(jnp.max(s_loc, axis=(-2, -1)), jnp.max(s_ctx, axis=-1))
        e_loc = jnp.exp(s_loc - m[..., None, None])
        e_ctx = jnp.exp(s_ctx - m[..., None])
        denom = jnp.sum(e_loc, axis=(-2, -1)) + jnp.sum(e_ctx, axis=-1)
        p_loc = (e_loc / denom[..., None, None]).astype(vi.dtype)
        p_ctx = (e_ctx / denom[..., None]).astype(vi.dtype)
        o = jnp.einsum('hrqkw,hrkwd->hrqd', p_loc, vr) + jnp.einsum('hrqc,hcd->hrqd', p_ctx, vci)
        return o.reshape(H, L, Dh)

    return lax.map(one_request, (q, k, v, k_ctx, v_ctx))


def _hier_moe(x, w_group, b_group, w_expert, b_expert, w_up, w_down):
    B, L, D = x.shape
    t = x.reshape(B * L, D)
    g_logits = (t @ w_group + b_group).astype(jnp.float32)
    g_prob = jax.nn.softmax(g_logits, axis=-1)
    g_sel = jnp.argmax(g_logits, axis=-1)
    g_w = jnp.take_along_axis(g_prob, g_sel[:, None], axis=-1)
    e_logits = (t @ w_expert + b_expert).astype(jnp.float32).reshape(-1, N_GROUPS, EXPERTS_PER_GROUP)
    e_logits = jnp.take_along_axis(e_logits, g_sel[:, None, None], axis=1)[:, 0]
    e_prob = jax.nn.softmax(e_logits, axis=-1)
    top_w, top_i = lax.top_k(e_prob, TOP_K)
    top_w = top_w / jnp.sum(top_w, axis=-1, keepdims=True)
    expert_id = g_sel[:, None] * EXPERTS_PER_GROUP + top_i
    combine = jnp.sum(jax.nn.one_hot(expert_id, N_EXPERTS, dtype=jnp.float32) * (g_w * top_w)[..., None], axis=1)
    combine = combine.astype(x.dtype)
    y = jnp.zeros_like(t)
    for e in range(N_EXPERTS):
        a, b = jnp.split(t @ w_up[e], 2, axis=-1)
        y = y + combine[:, e:e + 1] * ((jax.nn.silu(a) * b) @ w_down[e])
    return y.reshape(B, L, D)


def _normal(key, shape, scale):
    return jax.random.normal(key, shape, jnp.float32) * scale


def setup_inputs(seed: int = 0) -> dict:
    key = jax.random.key(seed)
    ks = jax.random.split(key, 40)
    D = D_MODEL
    nh, na = N_HYENA_LAYERS, N_ATTN_LAYERS
    min_decay = math.log(DECAY_TARGET) / SLOW_DECAY_PCT
    max_decay = math.log(DECAY_TARGET) / FAST_DECAY_PCT
    decay0 = jnp.tile(jnp.linspace(min_decay, max_decay, D, dtype=jnp.float32), 2)
    return {
        'x_prompt': _normal(ks[0], (BATCH, SEQ, D), 1.0),
        'x_sample': _normal(ks[1], (DEC_BATCH, DEC_SEQ, D), 1.0),
        'cache_k': _normal(ks[2], (DEC_BATCH, na, N_HEADS, PAST_LEN, HEAD_DIM), 1.0),
        'cache_v': _normal(ks[3], (DEC_BATCH, na, N_HEADS, PAST_LEN, HEAD_DIM), 1.0),
        'c': _normal(ks[4], (DEC_BATCH, D), 1.0),
        'c_ctx': _normal(ks[5], (D,), 1.0),
        'w_ada': _normal(ks[6], (DEPTH, D, 6 * D), 0.5 * D ** -0.5),
        'b_ada': _normal(ks[7], (DEPTH, 6 * D), 0.02),
        'ln_g': 1.0 + _normal(ks[8], (DEPTH, 2, D), 0.02),
        'ln_b': _normal(ks[9], (DEPTH, 2, D), 0.02),
        'hy_w_in': _normal(ks[10], (nh, D, 3 * D), D ** -0.5),
        'hy_b_in': _normal(ks[11], (nh, 3 * D), 0.02),
        'hy_conv_w': _normal(ks[12], (nh, SHORT_CONV, 3 * D), SHORT_CONV ** -0.5),
        'hy_conv_b': _normal(ks[13], (nh, 3 * D), 0.02),
        'hy_f_w1': _normal(ks[14], (nh, POS_EMB_DIM, FILTER_HIDDEN), POS_EMB_DIM ** -0.5),
        'hy_f_b1': _normal(ks[15], (nh, FILTER_HIDDEN), 0.02),
        'hy_f_w2': _normal(ks[16], (nh, FILTER_HIDDEN, FILTER_HIDDEN), FILTER_HIDDEN ** -0.5),
        'hy_f_b2': _normal(ks[17], (nh, FILTER_HIDDEN), 0.02),
        'hy_f_w3': _normal(ks[18], (nh, FILTER_HIDDEN, 2 * D), 0.1 * FILTER_HIDDEN ** -0.5),
        'hy_f_freq': 1.0 + _normal(ks[19], (nh, FILTER_HIDDEN), 0.02),
        'hy_decay': decay0[None] + _normal(ks[20], (nh, 2 * D), 0.1),
        'hy_d': _normal(ks[21], (nh, D), 0.1),
        'hy_w_out': _normal(ks[22], (nh, D, D), DEEPNORM_BETA * D ** -0.5),
        'hy_b_out': _normal(ks[23], (nh, D), 0.02),
        'na_w_qkv': _normal(ks[24], (na, D, 3 * D), D ** -0.5),
        'na_b_qkv': _normal(ks[25], (na, 3 * D), 0.02),
        'na_rpb': _normal(ks[26], (na, N_HEADS, 2 * WIN_ROWS - 1, 2 * WIN_COLS - 1), 0.02),
        'na_w_out': _normal(ks[27], (na, D, D), DEEPNORM_BETA * D ** -0.5),
        'na_b_out': _normal(ks[28], (na, D), 0.02),
        'moe_w_group': _normal(ks[29], (DEPTH, D, N_GROUPS), D ** -0.5),
        'moe_b_group': _normal(ks[30], (DEPTH, N_GROUPS), 0.01),
        'moe_w_expert': _normal(ks[31], (DEPTH, D, N_EXPERTS), D ** -0.5),
        'moe_b_expert': _normal(ks[32], (DEPTH, N_EXPERTS), 0.01),
        'moe_w_up': _normal(ks[33], (DEPTH, N_EXPERTS, D, 2 * D_EXPERT), D ** -0.5),
        'moe_w_down': _normal(ks[34], (DEPTH, N_EXPERTS, D_EXPERT, D), DEEPNORM_BETA * D_EXPERT ** -0.5),
    }


def reference(x_prompt, x_sample, cache_k, cache_v, c, c_ctx, w_ada, b_ada, ln_g, ln_b,
              hy_w_in, hy_b_in, hy_conv_w, hy_conv_b, hy_f_w1, hy_f_b1, hy_f_w2, hy_f_b2, hy_f_w3,
              hy_f_freq, hy_decay, hy_d, hy_w_out, hy_b_out,
              na_w_qkv, na_b_qkv, na_rpb, na_w_out, na_b_out,
              moe_w_group, moe_b_group, moe_w_expert, moe_b_expert, moe_w_up, moe_w_down):
    yp = x_prompt
    ys = x_sample
    cond_ctx = c_ctx[None, :]
    new_k_layers = []
    new_v_layers = []
    for i in range(DEPTH):
        j = i // N_MIXERS
        sm_p, cm_p, gm_p, sf_p, cf_p, gf_p = _adaln(cond_ctx, w_ada[i], b_ada[i])
        sm_s, cm_s, gm_s, sf_s, cf_s, gf_s = _adaln(c, w_ada[i], b_ada[i])
        hp = _modulate(yp, sm_p, cm_p)
        hs = _modulate(ys, sm_s, cm_s)
        if i % N_MIXERS == 0:
            hy = (hy_w_in[j], hy_b_in[j], hy_conv_w[j], hy_conv_b[j], hy_f_w1[j], hy_f_b1[j], hy_f_w2[j],
                  hy_f_b2[j], hy_f_w3[j], hy_f_freq[j], hy_decay[j], hy_d[j], hy_w_out[j], hy_b_out[j])
            op = _hyena(hp, *hy)
            os_ = _hyena(hs, *hy)
        else:
            qp, kp, vp = _qkv(hp, na_w_qkv[j], na_b_qkv[j])
            op = _merge_heads(_context_attention(qp, kp, vp)) @ na_w_out[j] + na_b_out[j]
            new_k_layers.append(kp)
            new_v_layers.append(vp)
            qs, ks_, vs = _qkv(hs, na_w_qkv[j], na_b_qkv[j])
            att = _neighbourhood_attention(qs, ks_, vs, cache_k[:, j], cache_v[:, j], na_rpb[j])
            os_ = _merge_heads(att) @ na_w_out[j] + na_b_out[j]
        yp = _layer_norm(DEEPNORM_ALPHA * yp + gm_p * op, ln_g[i, 0], ln_b[i, 0])
        ys = _layer_norm(DEEPNORM_ALPHA * ys + gm_s * os_, ln_g[i, 0], ln_b[i, 0])
        moe = (moe_w_group[i], moe_b_group[i], moe_w_expert[i], moe_b_expert[i], moe_w_up[i], moe_w_down[i])
        fp = _hier_moe(_modulate(yp, sf_p, cf_p), *moe)
        fs = _hier_moe(_modulate(ys, sf_s, cf_s), *moe)
        yp = _layer_norm(DEEPNORM_ALPHA * yp + gf_p * fp, ln_g[i, 1], ln_b[i, 1])
        ys = _layer_norm(DEEPNORM_ALPHA * ys + gf_s * fs, ln_g[i, 1], ln_b[i, 1])
    new_k = jnp.stack(new_k_layers, axis=1)
    new_v = jnp.stack(new_v_layers, axis=1)
    return (yp, ys, new_k, new_v)
```

```python
import functools
import math

import numpy as np
import jax
import jax.numpy as jnp
from jax import lax
from jax.experimental import pallas as pl
from jax.experimental.pallas import tpu as pltpu

F32 = jnp.float32
BF16 = jnp.bfloat16
HIGHEST = lax.Precision.HIGHEST

GRID_W = 64
N_BANDS = 16
N_HEADS = 16
WIN_ROWS = 8
WIN_COLS = 16
N_GROUPS = 4
EXPERTS_PER_GROUP = 4
LN_EPS = 1e-5

LANES = 128
VMEM_LIMIT_BYTES = 56 * 1024 * 1024

CONV_BLOCK = 256
DFT_N = 2 * CONV_BLOCK
N_FREQ = CONV_BLOCK + 1
FREQ_PAD = 264
FILTER_PAD = 128


def _cparams(sem):
    return pltpu.CompilerParams(dimension_semantics=sem, vmem_limit_bytes=VMEM_LIMIT_BYTES)


def _dot(a, b, precision=None):
    return jnp.dot(a, b, preferred_element_type=F32, precision=precision)


def _dot_nt(a, b):
    return lax.dot_general(a, b, (((1,), (1,)), ((), ())), preferred_element_type=F32)


def _silu(x):
    return x / (1.0 + jnp.exp(-x))


def _layer_norm(r, g, b):
    mu = jnp.mean(r, axis=-1, keepdims=True)
    d = r - mu
    var = jnp.mean(d * d, axis=-1, keepdims=True)
    return d * lax.rsqrt(var + LN_EPS) * g + b


def _adaln_body(c_ref, w_ref, b_ref, o_ref):
    o_ref[0] = _dot(_silu(c_ref[...]), w_ref[0], HIGHEST) + b_ref[0]


def _adaln(cond, w_ada, b_ada):
    depth, d, d6 = w_ada.shape
    n_cond = cond.shape[0]
    tn = 1024
    return pl.pallas_call(
        _adaln_body,
        grid=(depth, d6 // tn),
        in_specs=[
            pl.BlockSpec((n_cond, d), lambda l, j: (0, 0)),
            pl.BlockSpec((1, d, tn), lambda l, j: (l, 0, j)),
            pl.BlockSpec((1, 1, tn), lambda l, j: (l, 0, j)),
        ],
        out_specs=pl.BlockSpec((1, n_cond, tn), lambda l, j: (l, 0, j)),
        out_shape=jax.ShapeDtypeStruct((depth, n_cond, d6), F32),
        compiler_params=_cparams(("arbitrary", "arbitrary")),
        name="adaln",
    )(cond, w_ada, b_ada.reshape(depth, 1, d6))


class _Mod:
    def __init__(self, table, n_cond, rows_per_cond, n_latent_cond):
        self.table = table
        self.n_cond = n_cond
        self.rows_per_cond = rows_per_cond
        self.n_latent_cond = n_latent_cond

    def spec(self, layer, which, tm):
        base = (layer * 6 + which) * self.n_cond
        rpc, nl = self.rows_per_cond, self.n_latent_cond
        d = self.table.shape[-1]

        def index_map(i, *_):
            return (base + jnp.minimum((i * tm) // rpc, nl), 0, 0)

        return pl.BlockSpec((1, 1, d), index_map)


def _hyena_in_body(x_ref, sh_ref, sc_ref, w0_ref, w1_ref, w2_ref, b0_ref, b1_ref, b2_ref,
                   cw0_ref, cw1_ref, cw2_ref, cb0_ref, cb1_ref, cb2_ref,
                   x0_ref, u_ref, h_ref, *, n_latent_tiles, len_latent, len_ctx):
    i = pl.program_id(0)
    j = pl.program_id(1)
    tm = x_ref.shape[0]

    @pl.when(j == 0)
    def _():
        h_ref[...] = (x_ref[...] * (1.0 + sc_ref[0]) + sh_ref[0]).astype(BF16)

    seq_len = jnp.where(i < n_latent_tiles, len_latent, len_ctx)
    pos = lax.broadcasted_iota(jnp.int32, (tm, 1), 0) & (seq_len - 1)
    first = pos == 0
    last = pos == seq_len - 1

    def part(w_ref, b_ref, cw_ref, cb_ref):
        z = _dot(h_ref[...], w_ref[...]) + b_ref[...]
        z_prev = jnp.where(first, 0.0, pltpu.roll(z, 1, 0))
        z_next = jnp.where(last, 0.0, pltpu.roll(z, tm - 1, 0))
        cw = cw_ref[...]
        return cb_ref[...] + z_prev * cw[0:1] + z * cw[1:2] + z_next * cw[2:3]

    x0_ref[...] = part(w0_ref, b0_ref, cw0_ref, cb0_ref)
    x1 = part(w1_ref, b1_ref, cw1_ref, cb1_ref)
    v = part(w2_ref, b2_ref, cw2_ref, cb2_ref)
    u_ref[...] = v * x1


def _hyena_in(x, mod, layer, w_in, b_in, conv_w, conv_b, *, n_latent_rows, len_latent, len_ctx):
    t, d = x.shape
    tm, tn = len_latent, 256
    assert t % tm == 0 and tm % len_ctx == 0 and n_latent_rows % tm == 0
    assert len_latent & (len_latent - 1) == 0 and len_ctx & (len_ctx - 1) == 0
    nj = d // tn
    w_bf = w_in.astype(BF16)
    b2d = b_in.reshape(1, 3 * d)
    cb2d = conv_b.reshape(1, 3 * d)

    def col(p):
        return lambda i, j: (0, p * nj + j)

    body = functools.partial(_hyena_in_body, n_latent_tiles=n_latent_rows // tm,
                             len_latent=len_latent, len_ctx=len_ctx)
    out = jax.ShapeDtypeStruct((t, d), F32)
    return pl.pallas_call(
        body,
        grid=(t // tm, nj),
        in_specs=[pl.BlockSpec((tm, d), lambda i, j: (i, 0)),
                  mod.spec(layer, 0, tm), mod.spec(layer, 1, tm)]
        + [pl.BlockSpec((d, tn), col(p)) for p in range(3)]
        + [pl.BlockSpec((1, tn), col(p)) for p in range(3)]
        + [pl.BlockSpec((3, tn), col(p)) for p in range(3)]
        + [pl.BlockSpec((1, tn), col(p)) for p in range(3)],
        out_specs=[pl.BlockSpec((tm, tn), lambda i, j: (i, j))] * 2,
        out_shape=[out, out],
        scratch_shapes=[pltpu.VMEM((tm, d), BF16)],
        compiler_params=_cparams(("arbitrary", "arbitrary")),
        name="hyena_in",
    )(x, mod.table, mod.table, w_bf, w_bf, w_bf, b2d, b2d, b2d,
      conv_w, conv_w, conv_w, cb2d, cb2d, cb2d)


def _dft_matrices():
    f = np.arange(N_FREQ, dtype=np.float64)[:, None]
    m = np.arange(DFT_N, dtype=np.float64)[None, :]
    ang = 2.0 * np.pi * f * m / DFT_N
    fwd = np.zeros((2 * FREQ_PAD, DFT_N), np.float64)
    fwd[:N_FREQ] = np.cos(ang)
    fwd[FREQ_PAD:FREQ_PAD + N_FREQ] = -np.sin(ang)
    a = np.arange(CONV_BLOCK, dtype=np.float64)[:, None]
    fr = np.arange(N_FREQ, dtype=np.float64)[None, :]
    ang_i = 2.0 * np.pi * a * fr / DFT_N
    weight = np.full((1, N_FREQ), 2.0)
    weight[0, 0] = 1.0
    weight[0, N_FREQ - 1] = 1.0
    inv = np.zeros((CONV_BLOCK, 2 * FREQ_PAD), np.float64)
    inv[:, :N_FREQ] = weight * np.cos(ang_i) / DFT_N
    inv[:, FREQ_PAD:FREQ_PAD + N_FREQ] = -weight * np.sin(ang_i) / DFT_N
    return fwd.astype(np.float32), inv.astype(np.float32)


def _lag_tables(nb, seq_len):
    lag = np.zeros((2 * nb - 1, DFT_N, 1), np.float32)
    valid = np.zeros((2 * nb - 1, DFT_N, 1), np.float32)
    m = np.arange(DFT_N)
    for dd in range(2 * nb - 1):
        delta = dd - (nb - 1)
        lg = np.where(m < CONV_BLOCK, CONV_BLOCK * delta + m, CONV_BLOCK * delta + m - DFT_N)
        ok = (m != CONV_BLOCK) & (np.abs(lg) <= seq_len - 1)
        lag[dd, :, 0] = np.where(ok, lg, 0)
        valid[dd, :, 0] = ok
    return lag, valid


def _filter_body(lag_ref, valid_ref, bands_ref, w1t_ref, w1c_ref, w1s_ref, b1_ref, w2_ref, b2_ref,
                 freq_ref, w3f_ref, w3b_ref, dcf_ref, dcb_ref, dft_ref, k_ref, *, seq_len):
    lag = lag_ref[0]
    pos = jnp.abs(lag)
    tt = pos / float(max(seq_len - 1, 1))
    ang = (2.0 * math.pi / seq_len) * pos * bands_ref[...]
    freq = freq_ref[...]
    pre = (tt * w1t_ref[...] + _dot(jnp.cos(ang), w1c_ref[...], HIGHEST)
           + _dot(-jnp.sin(ang), w1s_ref[...], HIGHEST) + b1_ref[...])
    h = jnp.sin(freq * pre)
    h = jnp.sin(freq * (_dot(h, w2_ref[...], HIGHEST) + b2_ref[...]))
    hf = _dot(h, w3f_ref[...], HIGHEST) * jnp.exp(-tt * jnp.abs(dcf_ref[...]))
    hb = _dot(h, w3b_ref[...], HIGHEST) * jnp.exp(-tt * jnp.abs(dcb_ref[...]))
    taps = jnp.where(valid_ref[0] > 0.5, jnp.where(lag >= 0.0, hf, hb), 0.0)
    k_ref[0] = _dot(dft_ref[...], taps, HIGHEST)


def _filter_spectra(nb, seq_len, f_w1, f_b1, f_w2, f_b2, f_w3, f_freq, decay, dft_fwd):
    hid = f_w1.shape[1]
    d2 = f_w3.shape[1]
    d = d2 // 2
    ct = 256
    lag, valid = _lag_tables(nb, seq_len)
    ph = FILTER_PAD - hid
    bands = np.zeros((1, LANES), np.float32)
    bands[0, :N_BANDS] = np.linspace(1e-4, N_BANDS - 1, N_BANDS, dtype=np.float32)
    w1t = jnp.pad(f_w1[0:1], ((0, 0), (0, ph)))
    w1c = jnp.pad(f_w1[1:1 + N_BANDS], ((0, LANES - N_BANDS), (0, ph)))
    w1s = jnp.pad(f_w1[1 + N_BANDS:1 + 2 * N_BANDS], ((0, LANES - N_BANDS), (0, ph)))
    b1 = jnp.pad(f_b1.reshape(1, hid), ((0, 0), (0, ph)))
    w2 = jnp.pad(f_w2, ((0, ph), (0, ph)))
    b2 = jnp.pad(f_b2.reshape(1, hid), ((0, 0), (0, ph)))
    freq = jnp.pad(f_freq.reshape(1, hid), ((0, 0), (0, ph)))
    w3 = jnp.pad(f_w3, ((0, ph), (0, 0)))
    dec = decay.reshape(1, d2)
    nd = 2 * nb - 1
    nc = d // ct

    def whole(shape):
        return pl.BlockSpec(shape, lambda dd, c: (0,) * len(shape))

    return pl.pallas_call(
        functools.partial(_filter_body, seq_len=seq_len),
        grid=(nd, nc),
        in_specs=[
            pl.BlockSpec((1, DFT_N, 1), lambda dd, c: (dd, 0, 0)),
            pl.BlockSpec((1, DFT_N, 1), lambda dd, c: (dd, 0, 0)),
            whole((1, LANES)), whole((1, FILTER_PAD)), whole((LANES, FILTER_PAD)),
            whole((LANES, FILTER_PAD)), whole((1, FILTER_PAD)), whole((FILTER_PAD, FILTER_PAD)),
            whole((1, FILTER_PAD)), whole((1, FILTER_PAD)),
            pl.BlockSpec((FILTER_PAD, ct), lambda dd, c: (0, c)),
            pl.BlockSpec((FILTER_PAD, ct), lambda dd, c: (0, nc + c)),
            pl.BlockSpec((1, ct), lambda dd, c: (0, c)),
            pl.BlockSpec((1, ct), lambda dd, c: (0, nc + c)),
            whole((2 * FREQ_PAD, DFT_N)),
        ],
        out_specs=pl.BlockSpec((1, 2 * FREQ_PAD, ct), lambda dd, c: (dd, 0, c)),
        out_shape=jax.ShapeDtypeStruct((nd, 2 * FREQ_PAD, d), F32),
        compiler_params=_cparams(("arbitrary", "arbitrary")),
        name=f"hyena_filter_{seq_len}",
    )(jnp.asarray(lag), jnp.asarray(valid), jnp.asarray(bands), w1t, w1c, w1s, b1, w2, b2, freq,
      w3, w3, dec, dec, dft_fwd)


def _hyena_conv_body(u_ref, x0_ref, k_ref, dsk_ref, dftu_ref, idft_ref, o_ref, uf_ref, yf_ref,
                     *, n_seq, nb):
    fr_tile = 8
    n_ft = FREQ_PAD // fr_tile

    def fwd(blk, carry):
        r = pl.multiple_of(blk * CONV_BLOCK, CONV_BLOCK)
        uf_ref[blk] = _dot(dftu_ref[...], u_ref[pl.ds(r, CONV_BLOCK), :], HIGHEST)
        return carry

    lax.fori_loop(0, n_seq * nb, fwd, 0)

    def out_block(blk, carry):
        s = blk // nb
        i = blk - s * nb

        def freq_tile(ft, c2):
            r = pl.multiple_of(ft * fr_tile, fr_tile)
            acc_re = jnp.zeros((fr_tile, u_ref.shape[1]), F32)
            acc_im = jnp.zeros((fr_tile, u_ref.shape[1]), F32)
            for j in range(nb):
                dd = i - j + (nb - 1)
                k_re = k_ref[dd, pl.ds(r, fr_tile), :]
                k_im = k_ref[dd, pl.ds(FREQ_PAD + r, fr_tile), :]
                u_re = uf_ref[s * nb + j, pl.ds(r, fr_tile), :]
                u_im = uf_ref[s * nb + j, pl.ds(FREQ_PAD + r, fr_tile), :]
                acc_re = acc_re + (k_re * u_re - k_im * u_im)
                acc_im = acc_im + (k_re * u_im + k_im * u_re)
            yf_ref[pl.ds(r, fr_tile), :] = acc_re
            yf_ref[pl.ds(FREQ_PAD + r, fr_tile), :] = acc_im
            return c2

        lax.fori_loop(0, n_ft, freq_tile, 0)
        rows = pl.ds(pl.multiple_of(blk * CONV_BLOCK, CONV_BLOCK), CONV_BLOCK)
        y = _dot(idft_ref[...], yf_ref[...], HIGHEST)
        u_blk = u_ref[rows, :]
        o_ref[rows, :] = ((y + u_blk * dsk_ref[...]) * x0_ref[rows, :]).astype(o_ref.dtype)
        return carry

    lax.fori_loop(0, n_seq * nb, out_block, 0)


def _hyena_conv(u, x0, spectra, d_skip, dft_fwd, dft_inv, *, row_block0, n_steps, n_seq, nb, out):
    t, d = u.shape
    ct = 256
    tm = n_seq * nb * CONV_BLOCK
    nd = 2 * nb - 1
    body = functools.partial(_hyena_conv_body, n_seq=n_seq, nb=nb)
    rows = lambda c, b: (row_block0 + b, c)
    args = [u, x0, spectra, d_skip.reshape(1, d), dft_fwd[:, :CONV_BLOCK], dft_inv]
    in_specs = [
        pl.BlockSpec((tm, ct), rows),
        pl.BlockSpec((tm, ct), rows),
        pl.BlockSpec((nd, 2 * FREQ_PAD, ct), lambda c, b: (0, 0, c)),
        pl.BlockSpec((1, ct), lambda c, b: (0, c)),
        pl.BlockSpec((2 * FREQ_PAD, CONV_BLOCK), lambda c, b: (0, 0)),
        pl.BlockSpec((CONV_BLOCK, 2 * FREQ_PAD), lambda c, b: (0, 0)),
    ]
    aliases = {}
    if out is not None:
        args.append(out)
        in_specs.append(pl.BlockSpec(memory_space=pl.ANY))
        aliases = {len(args) - 1: 0}
        body = functools.partial(_drop_last_input, body, 6)
    return pl.pallas_call(
        body,
        grid=(d // ct, n_steps),
        in_specs=in_specs,
        out_specs=pl.BlockSpec((tm, ct), rows),
        out_shape=jax.ShapeDtypeStruct((t, d), BF16),
        input_output_aliases=aliases,
        scratch_shapes=[pltpu.VMEM((n_seq * nb, 2 * FREQ_PAD, ct), F32),
                        pltpu.VMEM((2 * FREQ_PAD, ct), F32)],
        compiler_params=_cparams(("arbitrary", "arbitrary")),
        name=f"hyena_conv_nb{nb}",
    )(*args)


def _drop_last_input(body, n_in, *refs):
    return body(*refs[:n_in], *refs[n_in + 1:])


def _out_proj_body(a_ref, w_ref, b_ref, x_ref, gate_ref, g_ref, beta_ref, o_ref, *, alpha):
    o = _dot(a_ref[...], w_ref[...]) + b_ref[...]
    r = alpha * x_ref[...] + gate_ref[0] * o
    o_ref[...] = _layer_norm(r, g_ref[...], beta_ref[...])


def _out_proj(a, w, b, x, mod, layer, ln_g, ln_b, alpha):
    t, d = x.shape
    tm = 512
    row = lambda i: (i, 0)
    const = lambda i: (0, 0)
    return pl.pallas_call(
        functools.partial(_out_proj_body, alpha=alpha),
        grid=(t // tm,),
        in_specs=[pl.BlockSpec((tm, d), row), pl.BlockSpec((d, d), const), pl.BlockSpec((1, d), const),
                  pl.BlockSpec((tm, d), row), mod.spec(layer, 2, tm),
                  pl.BlockSpec((1, d), const), pl.BlockSpec((1, d), const)],
        out_specs=pl.BlockSpec((tm, d), row),
        out_shape=jax.ShapeDtypeStruct((t, d), F32),
        compiler_params=_cparams(("arbitrary",)),
        name="out_proj_ln",
    )(a, w.astype(BF16), b.reshape(1, d), x, mod.table, ln_g.reshape(1, d), ln_b.reshape(1, d))


def _router_body(y_ref, sh_ref, sc_ref, w_ref, b_ref, t_ref, comb_ref):
    t = y_ref[...] * (1.0 + sc_ref[0]) + sh_ref[0]
    t_ref[...] = t.astype(BF16)
    lg = _dot(t, w_ref[...], HIGHEST) + b_ref[...]
    lane = lax.broadcasted_iota(jnp.int32, lg.shape, 1)
    neg = -jnp.inf
    far = jnp.int32(LANES)
    g_mask = lane < N_GROUPS
    gl = jnp.where(g_mask, lg, neg)
    g_max = jnp.max(gl, axis=-1, keepdims=True)
    g_sel = jnp.min(jnp.where(gl == g_max, lane, far), axis=-1, keepdims=True)
    g_w = 1.0 / jnp.sum(jnp.where(g_mask, jnp.exp(lg - g_max), 0.0), axis=-1, keepdims=True)
    e_lo = N_GROUPS + EXPERTS_PER_GROUP * g_sel
    e_mask = (lane >= e_lo) & (lane < e_lo + EXPERTS_PER_GROUP)
    el = jnp.where(e_mask, lg, neg)
    m1 = jnp.max(el, axis=-1, keepdims=True)
    i1 = jnp.min(jnp.where(el == m1, lane, far), axis=-1, keepdims=True)
    el2 = jnp.where(lane == i1, neg, el)
    m2 = jnp.max(el2, axis=-1, keepdims=True)
    i2 = jnp.min(jnp.where(el2 == m2, lane, far), axis=-1, keepdims=True)
    ratio = jnp.exp(m2 - m1)
    w1 = 1.0 / (1.0 + ratio)
    w2 = ratio / (1.0 + ratio)
    comb_ref[...] = (jnp.where(lane == i1 - N_GROUPS, g_w * w1, 0.0)
                     + jnp.where(lane == i2 - N_GROUPS, g_w * w2, 0.0))


def _router(y, mod, layer, w_group, b_group, w_expert, b_expert):
    t, d = y.shape
    tm = 512
    n_e = w_expert.shape[1]
    pad = LANES - N_GROUPS - n_e
    w = jnp.pad(jnp.concatenate([w_group, w_expert], axis=1), ((0, 0), (0, pad)))
    b = jnp.pad(jnp.concatenate([b_group, b_expert]), (0, pad)).reshape(1, LANES)
    row = lambda i: (i, 0)
    const = lambda i: (0, 0)
    return pl.pallas_call(
        _router_body,
        grid=(t // tm,),
        in_specs=[pl.BlockSpec((tm, d), row), mod.spec(layer, 3, tm), mod.spec(layer, 4, tm),
                  pl.BlockSpec((d, LANES), const), pl.BlockSpec((1, LANES), const)],
        out_specs=[pl.BlockSpec((tm, d), row), pl.BlockSpec((tm, LANES), row)],
        out_shape=[jax.ShapeDtypeStruct((t, d), BF16), jax.ShapeDtypeStruct((t, LANES), F32)],
        compiler_params=_cparams(("arbitrary",)),
        name="moe_router",
    )(y, mod.table, mod.table, w, b)


def _moe_body(t_ref, comb_ref, wu_ref, wd_ref, y_ref, gate_ref, g_ref, beta_ref, o_ref, acc_ref,
              *, alpha, d_expert):
    e = pl.program_id(1)

    @pl.when(e == 0)
    def _():
        acc_ref[...] = jnp.zeros_like(acc_ref)

    ab = _dot(t_ref[...], wu_ref[0])
    h = _silu(ab[:, :d_expert]) * ab[:, d_expert:]
    comb = comb_ref[...]
    lane = lax.broadcasted_iota(jnp.int32, comb.shape, 1)
    c = jnp.sum(jnp.where(lane == e, comb, 0.0), axis=-1, keepdims=True)
    acc_ref[...] += c * _dot(h.astype(BF16), wd_ref[0])

    @pl.when(e == pl.num_programs(1) - 1)
    def _():
        r = alpha * y_ref[...] + gate_ref[0] * acc_ref[...]
        o_ref[...] = _layer_norm(r, g_ref[...], beta_ref[...])


def _moe(tb, comb, w_up, w_down, y, mod, layer, ln_g, ln_b, alpha):
    t, d = y.shape
    n_e, _, d_up = w_up.shape
    d_expert = d_up // 2
    tm = 1024
    row = lambda i, e: (i, 0)
    const = lambda i, e: (0, 0)
    return pl.pallas_call(
        functools.partial(_moe_body, alpha=alpha, d_expert=d_expert),
        grid=(t // tm, n_e),
        in_specs=[pl.BlockSpec((tm, d), row), pl.BlockSpec((tm, LANES), row),
                  pl.BlockSpec((1, d, d_up), lambda i, e: (e, 0, 0)),
                  pl.BlockSpec((1, d_expert, d), lambda i, e: (e, 0, 0)),
                  pl.BlockSpec((tm, d), row), mod.spec(layer, 5, tm),
                  pl.BlockSpec((1, d), const), pl.BlockSpec((1, d), const)],
        out_specs=pl.BlockSpec((tm, d), row),
        out_shape=jax.ShapeDtypeStruct((t, d), F32),
        scratch_shapes=[pltpu.VMEM((tm, d), F32)],
        compiler_params=_cparams(("arbitrary", "arbitrary")),
        name="moe_experts_ln",
    )(tb, comb, w_up.astype(BF16), w_down.astype(BF16), y, mod.table,
      ln_g.reshape(1, d), ln_b.reshape(1, d))


def _qkv_body(y_ref, sh_ref, sc_ref, w_ref, b_ref, qkv_ref, nk_ref, nv_ref, *, n_latent_tiles, d):
    i = pl.program_id(0)
    h = (y_ref[...] * (1.0 + sc_ref[0]) + sh_ref[0]).astype(BF16)
    z = _dot(h, w_ref[...]) + b_ref[...]
    qkv_ref[...] = z.astype(BF16)

    @pl.when(i >= n_latent_tiles)
    def _():
        hd = nk_ref.shape[-1]
        for hh in range(nk_ref.shape[2]):
            nk_ref[0, 0, hh] = z[:, d + hh * hd:d + (hh + 1) * hd]
            nv_ref[0, 0, hh] = z[:, 2 * d + hh * hd:2 * d + (hh + 1) * hd]


def _qkv(y, mod, layer, w, b, *, n_latent_rows, n_ctx_batch, len_ctx):
    t, d = y.shape
    tm = len_ctx
    hd = d // N_HEADS
    n_lat = n_latent_rows // tm
    kv_map = lambda i: (jnp.maximum(i - n_lat, 0), 0, 0, 0, 0)
    kv_shape = jax.ShapeDtypeStruct((n_ctx_batch, 1, N_HEADS, len_ctx, hd), F32)
    return pl.pallas_call(
        functools.partial(_qkv_body, n_latent_tiles=n_lat, d=d),
        grid=(t // tm,),
        in_specs=[pl.BlockSpec((tm, d), lambda i: (i, 0)), mod.spec(layer, 0, tm), mod.spec(layer, 1, tm),
                  pl.BlockSpec((d, 3 * d), lambda i: (0, 0)), pl.BlockSpec((1, 3 * d), lambda i: (0, 0))],
        out_specs=[pl.BlockSpec((tm, 3 * d), lambda i: (i, 0)),
                   pl.BlockSpec((1, 1, N_HEADS, len_ctx, hd), kv_map),
                   pl.BlockSpec((1, 1, N_HEADS, len_ctx, hd), kv_map)],
        out_shape=[jax.ShapeDtypeStruct((t, 3 * d), BF16), kv_shape, kv_shape],
        compiler_params=_cparams(("arbitrary",)),
        name="attn_qkv",
    )(y, mod.table, mod.table, w.astype(BF16), b.reshape(1, 3 * d))


def _ctx_attn_body(qkv_ref, o_ref, *, d, scale):
    hd = d // N_HEADS
    for hh in range(N_HEADS):
        q = qkv_ref[:, hh * hd:(hh + 1) * hd]
        k = qkv_ref[:, d + hh * hd:d + (hh + 1) * hd]
        v = qkv_ref[:, 2 * d + hh * hd:2 * d + (hh + 1) * hd]
        s = _dot_nt(q, k) * scale
        e = jnp.exp(s - jnp.max(s, axis=-1, keepdims=True))
        p = (e / jnp.sum(e, axis=-1, keepdims=True)).astype(BF16)
        o_ref[:, hh * hd:(hh + 1) * hd] = _dot(p, v).astype(o_ref.dtype)


def _ctx_attn(qkv, att, *, row_block0, n_batch, len_ctx):
    t, d3 = qkv.shape
    d = d3 // 3
    scale = (d // N_HEADS) ** -0.5
    body = functools.partial(_drop_last_input, functools.partial(_ctx_attn_body, d=d, scale=scale), 1)
    return pl.pallas_call(
        body,
        grid=(n_batch,),
        in_specs=[pl.BlockSpec((len_ctx, d3), lambda b: (row_block0 + b, 0)),
                  pl.BlockSpec(memory_space=pl.ANY)],
        out_specs=pl.BlockSpec((len_ctx, d), lambda b: (row_block0 + b, 0)),
        out_shape=jax.ShapeDtypeStruct((t, d), BF16),
        input_output_aliases={1: 0},
        compiler_params=_cparams(("arbitrary",)),
        name="ctx_attention",
    )(qkv, att)


def _nbr_bias_table(rpb, rows):
    kh = min(WIN_ROWS, rows)
    col = np.arange(GRID_W)
    col_start = np.clip(col - WIN_COLS // 2, 0, GRID_W - WIN_COLS)
    in_band = (col[None, :] >= col_start[:, None]) & (col[None, :] < col_start[:, None] + WIN_COLS)
    dc_idx = np.clip(col[None, :] - col[:, None], -(WIN_COLS - 1), WIN_COLS - 1) + (WIN_COLS - 1)
    pat_rows = _pattern_rows(rows, kh)
    r = np.asarray(pat_rows)
    row_start = np.clip(r - kh // 2, 0, rows - kh)
    dr_idx = row_start[:, None] + np.arange(kh)[None, :] - r[:, None] + (WIN_ROWS - 1)
    bias = rpb[:, dr_idx[:, None, :, None], dc_idx[None, :, None, :]].astype(F32)
    bias = jnp.where(in_band[None, None, :, None, :], bias, -jnp.inf)
    return bias.reshape(rpb.shape[0], len(pat_rows), GRID_W, kh * GRID_W)


def _pattern_rows(rows, kh):
    lo = kh // 2
    hi = rows - kh + kh // 2
    return list(range(0, lo)) + [lo] + list(range(hi + 1, rows))


def _nbr_attn_body(q_ref, k_ref, v_ref, ck_ref, cv_ref, bias_ref, o_ref, *, rows, kh, scale):
    hd = ck_ref.shape[-1]
    lo = kh // 2
    hi = rows - kh + kh // 2
    n_keys = kh * GRID_W
    for hh in range(2):
        lanes = slice(hh * hd, (hh + 1) * hd)
        kc = ck_ref[0, 0, hh].astype(BF16)
        vc = cv_ref[0, 0, hh].astype(BF16)

        def row_body(r, carry):
            q = q_ref[pl.ds(pl.multiple_of(r * GRID_W, GRID_W), GRID_W), lanes]
            start = pl.multiple_of(jnp.clip(r - lo, 0, rows - kh) * GRID_W, GRID_W)
            kw = k_ref[pl.ds(start, n_keys), lanes]
            vw = v_ref[pl.ds(start, n_keys), lanes]
            pat = jnp.where(r < lo, r, jnp.where(r <= hi, lo, r - hi + lo))
            s_loc = _dot_nt(q, kw) * scale + bias_ref[hh, pat]
            s_ctx = _dot_nt(q, kc) * scale
            m = jnp.maximum(jnp.max(s_loc, axis=-1, keepdims=True), jnp.max(s_ctx, axis=-1, keepdims=True))
            e_loc = jnp.exp(s_loc - m)
            e_ctx = jnp.exp(s_ctx - m)
            denom = jnp.sum(e_loc, axis=-1, keepdims=True) + jnp.sum(e_ctx, axis=-1, keepdims=True)
            o = _dot(e_loc.astype(BF16), vw) + _dot(e_ctx.astype(BF16), vc)
            o_ref[pl.ds(pl.multiple_of(r * GRID_W, GRID_W), GRID_W), lanes] = (o / denom).astype(o_ref.dtype)
            return carry

        lax.fori_loop(0, rows, row_body, 0)


def _nbr_attn(qkv, cache_k, cache_v, bias, layer_j, *, n_batch, len_latent):
    t, d3 = qkv.shape
    d = d3 // 3
    hd = d // N_HEADS
    rows = len_latent // GRID_W
    kh = min(WIN_ROWS, rows)
    n_pat = bias.shape[1]
    n_hp = N_HEADS // 2
    pc = cache_k.shape[3]
    qmap = lambda p: (lambda b, h: (b, p * n_hp + h))
    cmap = lambda b, h: (b, layer_j, h, 0, 0)
    return pl.pallas_call(
        functools.partial(_nbr_attn_body, rows=rows, kh=kh, scale=hd ** -0.5),
        grid=(n_batch, n_hp),
        in_specs=[pl.BlockSpec((len_latent, 2 * hd), qmap(0)),
                  pl.BlockSpec((len_latent, 2 * hd), qmap(1)),
                  pl.BlockSpec((len_latent, 2 * hd), qmap(2)),
                  pl.BlockSpec((1, 1, 2, pc, hd), cmap),
                  pl.BlockSpec((1, 1, 2, pc, hd), cmap),
                  pl.BlockSpec((2, n_pat, GRID_W, kh * GRID_W), lambda b, h: (h, 0, 0, 0))],
        out_specs=pl.BlockSpec((len_latent, 2 * hd), lambda b, h: (b, h)),
        out_shape=jax.ShapeDtypeStruct((t, d), BF16),
        compiler_params=_cparams(("arbitrary", "arbitrary")),
        name="nbr_attention",
    )(qkv, qkv, qkv, cache_k, cache_v, bias)


def kernel(x_prompt, x_sample, cache_k, cache_v, c, c_ctx, w_ada, b_ada, ln_g, ln_b, hy_w_in, hy_b_in, hy_conv_w, hy_conv_b, hy_f_w1, hy_f_b1, hy_f_w2, hy_f_b2, hy_f_w3, hy_f_freq, hy_decay, hy_d, hy_w_out, hy_b_out, na_w_qkv, na_b_qkv, na_rpb, na_w_out, na_b_out, moe_w_group, moe_b_group, moe_w_expert, moe_b_expert, moe_w_up, moe_w_down):
    n_ctx_batch, len_ctx, d = x_prompt.shape
    n_lat_batch, len_latent, _ = x_sample.shape
    depth = w_ada.shape[0]
    alpha = (2 * depth) ** 0.25
    n_latent_rows = n_lat_batch * len_latent
    n_ctx_rows = n_ctx_batch * len_ctx
    nb_latent = len_latent // CONV_BLOCK
    assert len_ctx == CONV_BLOCK and len_latent % CONV_BLOCK == 0

    x = jnp.concatenate([x_sample.reshape(n_latent_rows, d), x_prompt.reshape(n_ctx_rows, d)], axis=0)

    n_cond = 16
    cond = jnp.concatenate([c, c_ctx[None, :], jnp.zeros((n_cond - n_lat_batch - 1, d), F32)], axis=0)
    mod_raw = _adaln(cond, w_ada, b_ada)
    table = mod_raw.reshape(depth, n_cond, 6, d).transpose(0, 2, 1, 3).reshape(depth * 6 * n_cond, 1, d)
    mod = _Mod(table, n_cond, len_latent, n_lat_batch)

    dft_fwd_np, dft_inv_np = _dft_matrices()
    dft_fwd = jnp.asarray(dft_fwd_np)
    dft_inv = jnp.asarray(dft_inv_np)

    new_k_layers, new_v_layers = [], []
    for i in range(depth):
        j = i // 2
        if i % 2 == 0:
            x0, u = _hyena_in(x, mod, i, hy_w_in[j], hy_b_in[j], hy_conv_w[j], hy_conv_b[j],
                              n_latent_rows=n_latent_rows, len_latent=len_latent, len_ctx=len_ctx)
            filt = (hy_f_w1[j], hy_f_b1[j], hy_f_w2[j], hy_f_b2[j], hy_f_w3[j], hy_f_freq[j], hy_decay[j])
            spec_lat = _filter_spectra(nb_latent, len_latent, *filt, dft_fwd)
            spec_ctx = _filter_spectra(1, len_ctx, *filt, dft_fwd)
            tm = len_latent
            n_seq_ctx = tm // len_ctx
            a = _hyena_conv(u, x0, spec_lat, hy_d[j], dft_fwd, dft_inv, row_block0=0,
                            n_steps=n_lat_batch, n_seq=1, nb=nb_latent, out=None)
            a = _hyena_conv(u, x0, spec_ctx, hy_d[j], dft_fwd, dft_inv, row_block0=n_lat_batch,
                            n_steps=n_ctx_rows // tm, n_seq=n_seq_ctx, nb=1, out=a)
            w_o, b_o = hy_w_out[j], hy_b_out[j]
        else:
            qkv, nk, nv = _qkv(x, mod, i, na_w_qkv[j], na_b_qkv[j], n_latent_rows=n_latent_rows,
                               n_ctx_batch=n_ctx_batch, len_ctx=len_ctx)
            new_k_layers.append(nk)
            new_v_layers.append(nv)
            bias = _nbr_bias_table(na_rpb[j], len_latent // GRID_W)
            a = _nbr_attn(qkv, cache_k, cache_v, bias, j, n_batch=n_lat_batch, len_latent=len_latent)
            a = _ctx_attn(qkv, a, row_block0=n_latent_rows // len_ctx, n_batch=n_ctx_batch, len_ctx=len_ctx)
            w_o, b_o = na_w_out[j], na_b_out[j]
        x = _out_proj(a, w_o, b_o, x, mod, i, ln_g[i, 0], ln_b[i, 0], alpha)
        tb, comb = _router(x, mod, i, moe_w_group[i], moe_b_group[i], moe_w_expert[i], moe_b_expert[i])
        x = _moe(tb, comb, moe_w_up[i], moe_w_down[i], x, mod, i, ln_g[i, 1], ln_b[i, 1], alpha)

    y_sample = x[:n_latent_rows].reshape(n_lat_batch, len_latent, d)
    y_prompt = x[n_latent_rows:].reshape(n_ctx_batch, len_ctx, d)
    new_k = jnp.concatenate(new_k_layers, axis=1)
    new_v = jnp.concatenate(new_v_layers, axis=1)
    return (y_prompt, y_sample, new_k, new_v)
```

```python
import functools
import math

import numpy as np
import jax
import jax.numpy as jnp
from jax import lax
from jax.experimental import pallas as pl
from jax.experimental.pallas import tpu as pltpu

F32 = jnp.float32
BF16 = jnp.bfloat16
HIGHEST = lax.Precision.HIGHEST

GRID_W = 64
N_BANDS = 16
N_HEADS = 16
WIN_ROWS = 8
WIN_COLS = 16
N_GROUPS = 4
EXPERTS_PER_GROUP = 4
LN_EPS = 1e-5

LANES = 128
SUBLANES = 8
BF16_ROWS = 16
VMEM_LIMIT_BYTES = 56 * 1024 * 1024

FILTER_PAD = 128
CONV_BLOCK_CTX = 256
CONV_BLOCK_LATENT = 512

MOE_TOKEN_TILE = 512
MOE_ROW_TILE = 512
NBR_QUERY_ROWS = 4


def _cparams(sem):
    return pltpu.CompilerParams(dimension_semantics=sem, vmem_limit_bytes=VMEM_LIMIT_BYTES)


def _dot(a, b, precision=None):
    return jnp.dot(a, b, preferred_element_type=F32, precision=precision)


def _dot_nt(a, b, precision=None):
    return lax.dot_general(a, b, (((1,), (1,)), ((), ())), preferred_element_type=F32,
                           precision=precision)


def _dot3(a, b):
    a_hi = a.astype(BF16)
    a_lo = (a - a_hi.astype(F32)).astype(BF16)
    b_hi = b.astype(BF16)
    b_lo = (b - b_hi.astype(F32)).astype(BF16)
    return _dot(a_hi, b_hi) + (_dot(a_hi, b_lo) + _dot(a_lo, b_hi))


def _silu(x):
    return x / (1.0 + jnp.exp(-x))


def _layer_norm(r, g, b):
    mu = jnp.mean(r, axis=-1, keepdims=True)
    d = r - mu
    var = jnp.mean(d * d, axis=-1, keepdims=True)
    return d * lax.rsqrt(var + LN_EPS) * g + b


def _round_up(x, m):
    return (x + m - 1) // m * m


def _adaln_body(c_ref, w_ref, b_ref, o_ref):
    o_ref[0] = _dot(_silu(c_ref[...]), w_ref[0], HIGHEST) + b_ref[0]


def _adaln(cond, w_ada, b_ada):
    depth, d, d6 = w_ada.shape
    n_cond = cond.shape[0]
    tn = 1024
    return pl.pallas_call(
        _adaln_body,
        grid=(depth, d6 // tn),
        in_specs=[
            pl.BlockSpec((n_cond, d), lambda l, j: (0, 0)),
            pl.BlockSpec((1, d, tn), lambda l, j: (l, 0, j)),
            pl.BlockSpec((1, 1, tn), lambda l, j: (l, 0, j)),
        ],
        out_specs=pl.BlockSpec((1, n_cond, tn), lambda l, j: (l, 0, j)),
        out_shape=jax.ShapeDtypeStruct((depth, n_cond, d6), F32),
        compiler_params=_cparams(("arbitrary", "arbitrary")),
        name="adaln",
    )(cond, w_ada, b_ada.reshape(depth, 1, d6))


class _Mod:
    def __init__(self, table, n_cond, rows_per_cond, n_latent_cond):
        self.table = table
        self.n_cond = n_cond
        self.rows_per_cond = rows_per_cond
        self.n_latent_cond = n_latent_cond

    def spec(self, layer, which, tm):
        base = (layer * 6 + which) * self.n_cond
        rpc, nl = self.rows_per_cond, self.n_latent_cond
        d = self.table.shape[-1]

        def index_map(i, *_):
            return (base + jnp.minimum((i * tm) // rpc, nl), 0, 0)

        return pl.BlockSpec((1, 1, d), index_map)


def _hyena_in_body(x_ref, sh_ref, sc_ref, w0_ref, w1_ref, w2_ref, b0_ref, b1_ref, b2_ref,
                   cw0_ref, cw1_ref, cw2_ref, cb0_ref, cb1_ref, cb2_ref,
                   x0_ref, u_ref, h_ref, *, n_latent_tiles, len_latent, len_ctx):
    i = pl.program_id(0)
    j = pl.program_id(1)
    tm = x_ref.shape[0]

    @pl.when(j == 0)
    def _():
        h_ref[...] = (x_ref[...] * (1.0 + sc_ref[0]) + sh_ref[0]).astype(BF16)

    seq_len = jnp.where(i < n_latent_tiles, len_latent, len_ctx)
    pos = lax.broadcasted_iota(jnp.int32, (tm, 1), 0) & (seq_len - 1)
    first = pos == 0
    last = pos == seq_len - 1

    def part(w_ref, b_ref, cw_ref, cb_ref):
        z = _dot(h_ref[...], w_ref[...]) + b_ref[...]
        z_prev = jnp.where(first, 0.0, pltpu.roll(z, 1, 0))
        z_next = jnp.where(last, 0.0, pltpu.roll(z, tm - 1, 0))
        cw = cw_ref[...]
        return cb_ref[...] + z_prev * cw[0:1] + z * cw[1:2] + z_next * cw[2:3]

    x0_ref[...] = part(w0_ref, b0_ref, cw0_ref, cb0_ref).astype(x0_ref.dtype)
    x1 = part(w1_ref, b1_ref, cw1_ref, cb1_ref)
    v = part(w2_ref, b2_ref, cw2_ref, cb2_ref)
    u_ref[...] = (v * x1).astype(u_ref.dtype)


def _hyena_in(x, mod, layer, w_in, b_in, conv_w, conv_b, *, n_latent_rows, len_latent, len_ctx):
    t, d = x.shape
    tm, tn = len_latent, 256
    assert t % tm == 0 and tm % len_ctx == 0 and n_latent_rows % tm == 0
    assert len_latent & (len_latent - 1) == 0 and len_ctx & (len_ctx - 1) == 0
    nj = d // tn
    w_bf = w_in.astype(BF16)
    b2d = b_in.reshape(1, 3 * d)
    cb2d = conv_b.reshape(1, 3 * d)

    def col(p):
        return lambda i, j: (0, p * nj + j)

    body = functools.partial(_hyena_in_body, n_latent_tiles=n_latent_rows // tm,
                             len_latent=len_latent, len_ctx=len_ctx)
    out = jax.ShapeDtypeStruct((t, d), BF16)
    return pl.pallas_call(
        body,
        grid=(t // tm, nj),
        in_specs=[pl.BlockSpec((tm, d), lambda i, j: (i, 0)),
                  mod.spec(layer, 0, tm), mod.spec(layer, 1, tm)]
        + [pl.BlockSpec((d, tn), col(p)) for p in range(3)]
        + [pl.BlockSpec((1, tn), col(p)) for p in range(3)]
        + [pl.BlockSpec((3, tn), col(p)) for p in range(3)]
        + [pl.BlockSpec((1, tn), col(p)) for p in range(3)],
        out_specs=[pl.BlockSpec((tm, tn), lambda i, j: (i, j))] * 2,
        out_shape=[out, out],
        scratch_shapes=[pltpu.VMEM((tm, d), BF16)],
        compiler_params=_cparams(("arbitrary", "arbitrary")),
        name="hyena_in",
    )(x, mod.table, mod.table, w_bf, w_bf, w_bf, b2d, b2d, b2d,
      conv_w, conv_w, conv_w, cb2d, cb2d, cb2d)


def _freq_pad(block):
    return _round_up(block + 1, SUBLANES)


def _dft_matrices(block):
    n = 2 * block
    nf = block + 1
    fp = _freq_pad(block)
    f = np.arange(nf, dtype=np.float64)[:, None]
    m = np.arange(n, dtype=np.float64)[None, :]
    ang = 2.0 * np.pi * f * m / n
    fwd = np.zeros((2 * fp, n), np.float64)
    fwd[:nf] = np.cos(ang)
    fwd[fp:fp + nf] = -np.sin(ang)
    a = np.arange(block, dtype=np.float64)[:, None]
    fr = np.arange(nf, dtype=np.float64)[None, :]
    ang_i = 2.0 * np.pi * a * fr / n
    weight = np.full((1, nf), 2.0)
    weight[0, 0] = 1.0
    weight[0, nf - 1] = 1.0
    inv = np.zeros((block, 2 * fp), np.float64)
    inv[:, :nf] = weight * np.cos(ang_i) / n
    inv[:, fp:fp + nf] = -weight * np.sin(ang_i) / n
    return fwd.astype(np.float32), inv.astype(np.float32)


def _lag_tables(nb, seq_len, block):
    n = 2 * block
    lag = np.zeros((2 * nb - 1, n, 1), np.float32)
    valid = np.zeros((2 * nb - 1, n, 1), np.float32)
    m = np.arange(n)
    for dd in range(2 * nb - 1):
        delta = dd - (nb - 1)
        lg = np.where(m < block, block * delta + m, block * delta + m - n)
        ok = (m != block) & (np.abs(lg) <= seq_len - 1)
        lag[dd, :, 0] = np.where(ok, lg, 0)
        valid[dd, :, 0] = ok
    return lag, valid


def _filter_body(lag_ref, valid_ref, bands_ref, w1t_ref, w1c_ref, w1s_ref, b1_ref, w2_ref, b2_ref,
                 freq_ref, w3_ref, dec_ref, dft_ref, k_ref, *, seq_len):
    d = k_ref.shape[-1]
    lag = lag_ref[0]
    pos = jnp.abs(lag)
    tt = pos / float(max(seq_len - 1, 1))
    ang = (2.0 * math.pi / seq_len) * pos * bands_ref[...]
    freq = freq_ref[...]
    pre = (tt * w1t_ref[...] + _dot(jnp.cos(ang), w1c_ref[...], HIGHEST)
           + _dot(-jnp.sin(ang), w1s_ref[...], HIGHEST) + b1_ref[...])
    h = jnp.sin(freq * pre)
    h = jnp.sin(freq * (_dot(h, w2_ref[...], HIGHEST) + b2_ref[...]))
    hf = _dot(h, w3_ref[:, :d], HIGHEST) * jnp.exp(-tt * jnp.abs(dec_ref[:, :d]))
    hb = _dot(h, w3_ref[:, d:], HIGHEST) * jnp.exp(-tt * jnp.abs(dec_ref[:, d:]))
    taps = jnp.where(valid_ref[0] > 0.5, jnp.where(lag >= 0.0, hf, hb), 0.0)
    k_ref[0] = _dot3(dft_ref[...], taps)


def _filter_spectra(nb, seq_len, block, f_w1, f_b1, f_w2, f_b2, f_w3, f_freq, decay, dft_fwd):
    hid = f_w1.shape[1]
    d2 = f_w3.shape[1]
    d = d2 // 2
    n = 2 * block
    fp2 = dft_fwd.shape[0]
    lag, valid = _lag_tables(nb, seq_len, block)
    ph = FILTER_PAD - hid
    bands = np.zeros((1, LANES), np.float32)
    bands[0, :N_BANDS] = np.linspace(1e-4, N_BANDS - 1, N_BANDS, dtype=np.float32)
    w1t = jnp.pad(f_w1[0:1], ((0, 0), (0, ph)))
    w1c = jnp.pad(f_w1[1:1 + N_BANDS], ((0, LANES - N_BANDS), (0, ph)))
    w1s = jnp.pad(f_w1[1 + N_BANDS:1 + 2 * N_BANDS], ((0, LANES - N_BANDS), (0, ph)))
    b1 = jnp.pad(f_b1.reshape(1, hid), ((0, 0), (0, ph)))
    w2 = jnp.pad(f_w2, ((0, ph), (0, ph)))
    b2 = jnp.pad(f_b2.reshape(1, hid), ((0, 0), (0, ph)))
    freq = jnp.pad(f_freq.reshape(1, hid), ((0, 0), (0, ph)))
    w3 = jnp.pad(f_w3, ((0, ph), (0, 0)))
    dec = decay.reshape(1, d2)
    nd = 2 * nb - 1

    def whole(shape):
        return pl.BlockSpec(shape, lambda dd: (0,) * len(shape))

    return pl.pallas_call(
        functools.partial(_filter_body, seq_len=seq_len),
        grid=(nd,),
        in_specs=[
            pl.BlockSpec((1, n, 1), lambda dd: (dd, 0, 0)),
            pl.BlockSpec((1, n, 1), lambda dd: (dd, 0, 0)),
            whole((1, LANES)), whole((1, FILTER_PAD)), whole((LANES, FILTER_PAD)),
            whole((LANES, FILTER_PAD)), whole((1, FILTER_PAD)), whole((FILTER_PAD, FILTER_PAD)),
            whole((1, FILTER_PAD)), whole((1, FILTER_PAD)),
            whole((FILTER_PAD, d2)), whole((1, d2)), whole((fp2, n)),
        ],
        out_specs=pl.BlockSpec((1, fp2, d), lambda dd: (dd, 0, 0)),
        out_shape=jax.ShapeDtypeStruct((nd, fp2, d), F32),
        compiler_params=_cparams(("arbitrary",)),
        name=f"hyena_filter_{seq_len}",
    )(jnp.asarray(lag), jnp.asarray(valid), jnp.asarray(bands), w1t, w1c, w1s, b1, w2, b2, freq,
      w3, dec, dft_fwd)


def _hyena_conv_body(u_ref, x0_ref, k_ref, dsk_ref, dftu_ref, idft_ref, o_ref, uf_ref, yf_ref,
                     *, n_seq, nb, block):
    fp = k_ref.shape[1] // 2
    ft_rows = SUBLANES

    def fwd(blk, carry):
        rows = pl.ds(pl.multiple_of(blk * block, block), block)
        uf_ref[blk] = _dot(dftu_ref[...], u_ref[rows, :])
        return carry

    lax.fori_loop(0, n_seq * nb, fwd, 0)

    def freq_tile(ft, carry):
        r = pl.multiple_of(ft * ft_rows, ft_rows)
        re = pl.ds(r, ft_rows)
        im = pl.ds(fp + r, ft_rows)
        k_re = [k_ref[dd, re, :] for dd in range(2 * nb - 1)]
        k_im = [k_ref[dd, im, :] for dd in range(2 * nb - 1)]
        for s in range(n_seq):
            u_re = [uf_ref[s * nb + j, re, :] for j in range(nb)]
            u_im = [uf_ref[s * nb + j, im, :] for j in range(nb)]
            for i in range(nb):
                acc_re = None
                acc_im = None
                for j in range(nb):
                    dd = i - j + (nb - 1)
                    t_re = k_re[dd] * u_re[j] - k_im[dd] * u_im[j]
                    t_im = k_re[dd] * u_im[j] + k_im[dd] * u_re[j]
                    acc_re = t_re if acc_re is None else acc_re + t_re
                    acc_im = t_im if acc_im is None else acc_im + t_im
                yf_ref[s * nb + i, re, :] = acc_re
                yf_ref[s * nb + i, im, :] = acc_im
        return carry

    lax.fori_loop(0, fp // ft_rows, freq_tile, 0)

    def inv(blk, carry):
        rows = pl.ds(pl.multiple_of(blk * block, block), block)
        y = _dot(idft_ref[...], yf_ref[blk].astype(BF16))
        u_blk = u_ref[rows, :].astype(F32)
        o_ref[rows, :] = ((y + u_blk * dsk_ref[...]) * x0_ref[rows, :].astype(F32)).astype(o_ref.dtype)
        return carry

    lax.fori_loop(0, n_seq * nb, inv, 0)


def _drop_input(body, idx, *refs):
    return body(*refs[:idx], *refs[idx + 1:])


def _hyena_conv(u, x0, spectra, d_skip, dft_fwd, dft_inv, *, block, row_block0, n_steps, n_seq, nb, out):
    t, d = u.shape
    ct = 256
    tm = n_seq * nb * block
    nd = 2 * nb - 1
    fp2 = dft_fwd.shape[0]
    body = functools.partial(_hyena_conv_body, n_seq=n_seq, nb=nb, block=block)
    rows = lambda c, b: (row_block0 + b, c)
    args = [u, x0, spectra, d_skip.reshape(1, d), dft_fwd[:, :block].astype(BF16), dft_inv.astype(BF16)]
    in_specs = [
        pl.BlockSpec((tm, ct), rows),
        pl.BlockSpec((tm, ct), rows),
        pl.BlockSpec((nd, fp2, ct), lambda c, b: (0, 0, c)),
        pl.BlockSpec((1, ct), lambda c, b: (0, c)),
        pl.BlockSpec((fp2, block), lambda c, b: (0, 0)),
        pl.BlockSpec((block, fp2), lambda c, b: (0, 0)),
    ]
    aliases = {}
    if out is not None:
        aliases = {len(args): 0}
        args.append(out)
        in_specs.append(pl.BlockSpec(memory_space=pl.ANY))
        body = functools.partial(_drop_input, body, 6)
    return pl.pallas_call(
        body,
        grid=(d // ct, n_steps),
        in_specs=in_specs,
        out_specs=pl.BlockSpec((tm, ct), rows),
        out_shape=jax.ShapeDtypeStruct((t, d), BF16),
        input_output_aliases=aliases,
        scratch_shapes=[pltpu.VMEM((n_seq * nb, fp2, ct), F32),
                        pltpu.VMEM((n_seq * nb, fp2, ct), F32)],
        compiler_params=_cparams(("arbitrary", "arbitrary")),
        name=f"hyena_conv_nb{nb}",
    )(*args)


def _out_proj_body(a_ref, w_ref, b_ref, x_ref, gate_ref, g_ref, beta_ref, o_ref, *, alpha):
    o = _dot(a_ref[...], w_ref[...]) + b_ref[...]
    r = alpha * x_ref[...] + gate_ref[0] * o
    o_ref[...] = _layer_norm(r, g_ref[...], beta_ref[...])


def _out_proj(a, w, b, x, mod, layer, ln_g, ln_b, alpha):
    t, d = x.shape
    tm = 512
    row = lambda i: (i, 0)
    const = lambda i: (0, 0)
    return pl.pallas_call(
        functools.partial(_out_proj_body, alpha=alpha),
        grid=(t // tm,),
        in_specs=[pl.BlockSpec((tm, d), row), pl.BlockSpec((d, d), const), pl.BlockSpec((1, d), const),
                  pl.BlockSpec((tm, d), row), mod.spec(layer, 2, tm),
                  pl.BlockSpec((1, d), const), pl.BlockSpec((1, d), const)],
        out_specs=pl.BlockSpec((tm, d), row),
        out_shape=jax.ShapeDtypeStruct((t, d), F32),
        compiler_params=_cparams(("arbitrary",)),
        name="out_proj_ln",
    )(a, w.astype(BF16), b.reshape(1, d), x, mod.table, ln_g.reshape(1, d), ln_b.reshape(1, d))


def _router_body(y_ref, sh_ref, sc_ref, w_ref, b_ref, t_ref, route_ref, cnt_ref):
    t = y_ref[...] * (1.0 + sc_ref[0]) + sh_ref[0]
    t_ref[...] = t.astype(BF16)
    lg = _dot(t, w_ref[...], HIGHEST) + b_ref[...]
    lane = lax.broadcasted_iota(jnp.int32, lg.shape, 1)
    neg = -jnp.inf
    far = jnp.int32(LANES)
    g_mask = lane < N_GROUPS
    gl = jnp.where(g_mask, lg, neg)
    g_max = jnp.max(gl, axis=-1, keepdims=True)
    g_sel = jnp.min(jnp.where(gl == g_max, lane, far), axis=-1, keepdims=True)
    g_w = 1.0 / jnp.sum(jnp.where(g_mask, jnp.exp(lg - g_max), 0.0), axis=-1, keepdims=True)
    e_lo = N_GROUPS + EXPERTS_PER_GROUP * g_sel
    e_mask = (lane >= e_lo) & (lane < e_lo + EXPERTS_PER_GROUP)
    el = jnp.where(e_mask, lg, neg)
    m1 = jnp.max(el, axis=-1, keepdims=True)
    i1 = jnp.min(jnp.where(el == m1, lane, far), axis=-1, keepdims=True)
    el2 = jnp.where(lane == i1, neg, el)
    m2 = jnp.max(el2, axis=-1, keepdims=True)
    i2 = jnp.min(jnp.where(el2 == m2, lane, far), axis=-1, keepdims=True)
    ratio = jnp.exp(m2 - m1)
    w1 = 1.0 / (1.0 + ratio)
    w2 = ratio / (1.0 + ratio)
    e1 = i1 - N_GROUPS
    e2 = i2 - N_GROUPS
    route_ref[...] = jnp.where(lane == 0, e1.astype(F32),
                               jnp.where(lane == 1, e2.astype(F32),
                                         jnp.where(lane == 2, g_w * w1,
                                                   jnp.where(lane == 3, g_w * w2, 0.0))))
    hit = ((lane == e1) | (lane == e2)).astype(F32)
    cnt_ref[0] = jnp.broadcast_to(jnp.sum(hit, axis=0, keepdims=True), cnt_ref.shape[1:])


def _router(y, mod, layer, w_group, b_group, w_expert, b_expert):
    t, d = y.shape
    tm = MOE_TOKEN_TILE
    n_e = w_expert.shape[1]
    pad = LANES - N_GROUPS - n_e
    w = jnp.pad(jnp.concatenate([w_group, w_expert], axis=1), ((0, 0), (0, pad)))
    b = jnp.pad(jnp.concatenate([b_group, b_expert]), (0, pad)).reshape(1, LANES)
    row = lambda i: (i, 0)
    const = lambda i: (0, 0)
    return pl.pallas_call(
        _router_body,
        grid=(t // tm,),
        in_specs=[pl.BlockSpec((tm, d), row), mod.spec(layer, 3, tm), mod.spec(layer, 4, tm),
                  pl.BlockSpec((d, LANES), const), pl.BlockSpec((1, LANES), const)],
        out_specs=[pl.BlockSpec((tm, d), row), pl.BlockSpec((tm, LANES), row),
                   pl.BlockSpec((1, SUBLANES, LANES), lambda i: (i, 0, 0))],
        out_shape=[jax.ShapeDtypeStruct((t, d), BF16), jax.ShapeDtypeStruct((t, LANES), F32),
                   jax.ShapeDtypeStruct((t // tm, SUBLANES, LANES), F32)],
        compiler_params=_cparams(("arbitrary",)),
        name="moe_router",
    )(y, mod.table, mod.table, w, b)


def _moe_layout(counts, n_experts, n_row_tiles_max):
    cnt = counts[:, 0, :n_experts].astype(jnp.int32)
    run_len = (cnt + (BF16_ROWS - 1)) // BF16_ROWS * BF16_ROWS
    total = jnp.sum(run_len, axis=0)
    region = (total + (MOE_ROW_TILE - 1)) // MOE_ROW_TILE * MOE_ROW_TILE
    region_end = jnp.cumsum(region)
    expert_start = region_end - region
    run_start = expert_start[None, :] + jnp.cumsum(run_len, axis=0) - run_len
    n_used = (region_end[-1] // MOE_ROW_TILE).reshape(1).astype(jnp.int32)
    tile_idx = jnp.arange(n_row_tiles_max, dtype=jnp.int32)
    tile_expert = jnp.minimum(
        jnp.sum((tile_idx[:, None] >= (region_end // MOE_ROW_TILE)[None, :]).astype(jnp.int32), axis=1),
        n_experts - 1).astype(jnp.int32)
    return dict(run_start=run_start.reshape(-1).astype(jnp.int32),
                run_len=run_len.reshape(-1).astype(jnp.int32),
                gap_start=(expert_start + total).astype(jnp.int32),
                gap_len=(region - total).astype(jnp.int32),
                tile_expert=tile_expert, n_used=n_used)


def _local_positions(route, run_len_ref, tile, n_experts):
    tm = route.shape[0]
    lane = lax.broadcasted_iota(jnp.int32, (tm, LANES), 1)
    e1 = route[:, 0:1].astype(jnp.int32)
    e2 = route[:, 1:2].astype(jnp.int32)
    hit = ((lane == e1) | (lane == e2)).astype(BF16)
    earlier = (lax.broadcasted_iota(jnp.int32, (tm, tm), 1)
               < lax.broadcasted_iota(jnp.int32, (tm, tm), 0)).astype(BF16)
    rank = _dot(earlier, hit)
    lane1 = lax.broadcasted_iota(jnp.int32, (1, LANES), 1)
    starts = jnp.zeros((1, LANES), F32)
    off = jnp.int32(0)
    for e in range(n_experts):
        starts = jnp.where(lane1 == e, off.astype(F32), starts)
        off = off + run_len_ref[tile * n_experts + e]
    pos = rank + starts
    p1 = jnp.sum(jnp.where(lane == e1, pos, 0.0), axis=-1, keepdims=True)
    p2 = jnp.sum(jnp.where(lane == e2, pos, 0.0), axis=-1, keepdims=True)
    return p1, p2


def _chunk_copies(src_ref, src_off, dst_ref, dst_off, n, sem, max_chunk, advance_src=True):
    out = []
    off = jnp.int32(0)
    bit = max_chunk
    while bit >= BF16_ROWS:
        take = n & bit
        s = pl.multiple_of(src_off + off, BF16_ROWS) if advance_src else src_off
        t = pl.multiple_of(dst_off + off, BF16_ROWS)
        cp = pltpu.make_async_copy(src_ref.at[pl.ds(s, bit)], dst_ref.at[pl.ds(t, bit)], sem)
        out.append((take != 0, cp))
        off = off + take
        bit //= 2
    return out


def _start_all(copies):
    for cond, cp in copies:
        pl.when(cond)(cp.start)


def _wait_all(copies):
    for cond, cp in copies:
        pl.when(cond)(cp.wait)


def _dispatch_body(rs_ref, rl_ref, gs_ref, gl_ref, t_ref, route_ref, xg_ref, sorted_ref, zero_ref, sem,
                   *, n_experts):
    tile = pl.program_id(0)
    tm = t_ref.shape[0]
    n_rows = sorted_ref.shape[0]
    route = route_ref[...]
    p1, p2 = _local_positions(route, rl_ref, tile, n_experts)
    lane = lax.broadcasted_iota(jnp.int32, (tm, LANES), 1)
    both = jnp.where(lane == 0, p1, jnp.where(lane == 1, p2, 0.0))
    pick = (lax.broadcasted_iota(jnp.int32, (SUBLANES, LANES), 0)
            == lax.broadcasted_iota(jnp.int32, (SUBLANES, LANES), 1)).astype(F32)
    as_rows = _dot_nt(pick, both, HIGHEST)
    r_iota = lax.broadcasted_iota(jnp.int32, (n_rows, tm), 0).astype(F32)
    select = ((r_iota == as_rows[0:1]) | (r_iota == as_rows[1:2])).astype(BF16)
    sorted_ref[...] = _dot(select, t_ref[...]).astype(BF16)

    copies = []
    off = jnp.int32(0)
    for e in range(n_experts):
        n = rl_ref[tile * n_experts + e]
        copies += _chunk_copies(sorted_ref, off, xg_ref, rs_ref[tile * n_experts + e], n, sem, tm)
        off = off + n
    _start_all(copies)
    _wait_all(copies)

    @pl.when(tile == pl.num_programs(0) - 1)
    def _():
        zero_ref[...] = jnp.zeros_like(zero_ref)
        fills = []
        for e in range(n_experts):
            fills += _chunk_copies(zero_ref, 0, xg_ref, gs_ref[e], gl_ref[e], sem, zero_ref.shape[0],
                                   advance_src=False)
        _start_all(fills)
        _wait_all(fills)


def _sorted_rows(tm, n_experts):
    return _round_up(2 * tm + n_experts * (BF16_ROWS - 1), LANES)


def _dispatch(tb, route, layout, n_experts, n_rows_max):
    t, d = tb.shape
    tm = MOE_TOKEN_TILE
    row = lambda i, *_: (i, 0)
    grid_spec = pltpu.PrefetchScalarGridSpec(
        num_scalar_prefetch=4,
        grid=(t // tm,),
        in_specs=[pl.BlockSpec((tm, d), row), pl.BlockSpec((tm, LANES), row)],
        out_specs=pl.BlockSpec(memory_space=pl.ANY),
        scratch_shapes=[pltpu.VMEM((_sorted_rows(tm, n_experts), d), BF16),
                        pltpu.VMEM((MOE_ROW_TILE // 2, d), BF16),
                        pltpu.SemaphoreType.DMA(())],
    )
    return pl.pallas_call(
        functools.partial(_dispatch_body, n_experts=n_experts),
        grid_spec=grid_spec,
        out_shape=jax.ShapeDtypeStruct((n_rows_max, d), BF16),
        compiler_params=_cparams(("arbitrary",)),
        name="moe_dispatch",
    )(layout["run_start"], layout["run_len"], layout["gap_start"], layout["gap_len"], tb, route)


def _experts_body(te_ref, nu_ref, x_ref, wu_ref, wd_ref, y_ref, wub_ref, wdb_ref, *, d_expert):
    g = pl.program_id(0)

    @pl.when(g < nu_ref[0])
    def _():
        prev = te_ref[jnp.maximum(g - 1, 0)]

        @pl.when((g == 0) | (te_ref[g] != prev))
        def _():
            wub_ref[...] = wu_ref[0].astype(BF16)
            wdb_ref[...] = wd_ref[0].astype(BF16)

        ab = _dot(x_ref[...], wub_ref[...])
        h = _silu(ab[:, :d_expert]) * ab[:, d_expert:]
        y_ref[...] = _dot(h.astype(BF16), wdb_ref[...]).astype(y_ref.dtype)


def _experts(xg, w_up, w_down, layout):
    n_rows, d = xg.shape
    n_e, _, d_up = w_up.shape
    d_expert = d_up // 2
    tm = MOE_ROW_TILE
    used = lambda g, te, nu: jnp.minimum(g, nu[0] - 1)
    grid_spec = pltpu.PrefetchScalarGridSpec(
        num_scalar_prefetch=2,
        grid=(n_rows // tm,),
        in_specs=[pl.BlockSpec((tm, d), lambda g, te, nu: (used(g, te, nu), 0)),
                  pl.BlockSpec((1, d, d_up), lambda g, te, nu: (te[used(g, te, nu)], 0, 0)),
                  pl.BlockSpec((1, d_expert, d), lambda g, te, nu: (te[used(g, te, nu)], 0, 0))],
        out_specs=pl.BlockSpec((tm, d), lambda g, te, nu: (used(g, te, nu), 0)),
        scratch_shapes=[pltpu.VMEM((d, d_up), BF16), pltpu.VMEM((d_expert, d), BF16)],
    )
    return pl.pallas_call(
        functools.partial(_experts_body, d_expert=d_expert),
        grid_spec=grid_spec,
        out_shape=jax.ShapeDtypeStruct((n_rows, d), BF16),
        compiler_params=_cparams(("arbitrary",)),
        name="moe_experts",
    )(layout["tile_expert"], layout["n_used"], xg, w_up, w_down)


def _combine_body(rs_ref, rl_ref, yg_ref, route_ref, x_ref, gate_ref, g_ref, beta_ref, *rest,
                  alpha, n_experts, split_tile):
    if split_tile is None:
        o_ref, ybuf_ref, sem = rest
    else:
        o_ref, o2_ref, ybuf_ref, sem = rest
    tile = pl.program_id(0)
    tm = x_ref.shape[0]
    n_rows = ybuf_ref.shape[0]
    ybuf_ref[2 * tm:, :] = jnp.zeros((n_rows - 2 * tm, ybuf_ref.shape[1]), ybuf_ref.dtype)

    copies = []
    off = jnp.int32(0)
    for e in range(n_experts):
        n = rl_ref[tile * n_experts + e]
        copies += _chunk_copies(yg_ref, rs_ref[tile * n_experts + e], ybuf_ref, off, n, sem, tm)
        off = off + n
    _start_all(copies)

    route = route_ref[...]
    p1, p2 = _local_positions(route, rl_ref, tile, n_experts)
    r_lane = lax.broadcasted_iota(jnp.int32, (tm, n_rows), 1).astype(F32)
    cmat = (jnp.where(r_lane == p1, route[:, 2:3], 0.0)
            + jnp.where(r_lane == p2, route[:, 3:4], 0.0)).astype(BF16)
    _wait_all(copies)
    y = _dot(cmat, ybuf_ref[...])
    r = alpha * x_ref[...] + gate_ref[0] * y
    res = _layer_norm(r, g_ref[...], beta_ref[...])
    if split_tile is None:
        o_ref[...] = res
    else:
        @pl.when(tile < split_tile)
        def _():
            o_ref[...] = res

        @pl.when(tile >= split_tile)
        def _():
            o2_ref[...] = res


def _combine(yg, route, x, layout, mod, layer, ln_g, ln_b, alpha, n_experts, split_rows=None):
    t, d = x.shape
    tm = MOE_TOKEN_TILE
    row = lambda i, *_: (i, 0)
    const = lambda i, *_: (0, 0)
    if split_rows is None:
        split_tile = None
        out_specs = pl.BlockSpec((tm, d), row)
        out_shape = jax.ShapeDtypeStruct((t, d), F32)
    else:
        assert split_rows % tm == 0
        split_tile = split_rows // tm
        out_specs = [pl.BlockSpec((tm, d), lambda i, *_: (jnp.minimum(i, split_tile - 1), 0)),
                     pl.BlockSpec((tm, d), lambda i, *_: (jnp.maximum(i - split_tile, 0), 0))]
        out_shape = [jax.ShapeDtypeStruct((split_rows, d), F32),
                     jax.ShapeDtypeStruct((t - split_rows, d), F32)]
    grid_spec = pltpu.PrefetchScalarGridSpec(
        num_scalar_prefetch=2,
        grid=(t // tm,),
        in_specs=[pl.BlockSpec(memory_space=pl.ANY), pl.BlockSpec((tm, LANES), row),
                  pl.BlockSpec((tm, d), row), mod.spec(layer, 5, tm),
                  pl.BlockSpec((1, d), const), pl.BlockSpec((1, d), const)],
        out_specs=out_specs,
        scratch_shapes=[pltpu.VMEM((_sorted_rows(tm, n_experts), d), BF16),
                        pltpu.SemaphoreType.DMA(())],
    )
    return pl.pallas_call(
        functools.partial(_combine_body, alpha=alpha, n_experts=n_experts, split_tile=split_tile),
        grid_spec=grid_spec,
        out_shape=out_shape,
        compiler_params=_cparams(("arbitrary",)),
        name="moe_combine_ln",
    )(layout["run_start"], layout["run_len"], yg, route, x, mod.table,
      ln_g.reshape(1, d), ln_b.reshape(1, d))


def _moe_layer(x, mod, layer, w_group, b_group, w_expert, b_expert, w_up, w_down, ln_g, ln_b, alpha,
               split_rows=None):
    t, d = x.shape
    n_e = w_up.shape[0]
    n_tok_tiles = t // MOE_TOKEN_TILE
    max_rows = 2 * t + n_tok_tiles * n_e * (BF16_ROWS - 1) + n_e * (MOE_ROW_TILE - BF16_ROWS)
    n_rows_max = _round_up(max_rows, MOE_ROW_TILE)
    tb, route, counts = _router(x, mod, layer, w_group, b_group, w_expert, b_expert)
    layout = _moe_layout(counts, n_e, n_rows_max // MOE_ROW_TILE)
    xg = _dispatch(tb, route, layout, n_e, n_rows_max)
    yg = _experts(xg, w_up, w_down, layout)
    return _combine(yg, route, x, layout, mod, layer, ln_g, ln_b, alpha, n_e, split_rows)


def _qkv_body(y_ref, sh_ref, sc_ref, w_ref, b_ref, qkv_ref, nk_ref, nv_ref, *, n_latent_tiles, d):
    i = pl.program_id(0)
    h = (y_ref[...] * (1.0 + sc_ref[0]) + sh_ref[0]).astype(BF16)
    z = _dot(h, w_ref[...]) + b_ref[...]
    qkv_ref[...] = z.astype(BF16)

    @pl.when(i >= n_latent_tiles)
    def _():
        hd = nk_ref.shape[-1]
        for hh in range(nk_ref.shape[2]):
            nk_ref[0, 0, hh] = z[:, d + hh * hd:d + (hh + 1) * hd]
            nv_ref[0, 0, hh] = z[:, 2 * d + hh * hd:2 * d + (hh + 1) * hd]


def _qkv(y, mod, layer, w, b, *, n_latent_rows, n_ctx_batch, len_ctx):
    t, d = y.shape
    tm = len_ctx
    hd = d // N_HEADS
    n_lat = n_latent_rows // tm
    kv_map = lambda i: (jnp.maximum(i - n_lat, 0), 0, 0, 0, 0)
    kv_shape = jax.ShapeDtypeStruct((n_ctx_batch, 1, N_HEADS, len_ctx, hd), F32)
    return pl.pallas_call(
        functools.partial(_qkv_body, n_latent_tiles=n_lat, d=d),
        grid=(t // tm,),
        in_specs=[pl.BlockSpec((tm, d), lambda i: (i, 0)), mod.spec(layer, 0, tm), mod.spec(layer, 1, tm),
                  pl.BlockSpec((d, 3 * d), lambda i: (0, 0)), pl.BlockSpec((1, 3 * d), lambda i: (0, 0))],
        out_specs=[pl.BlockSpec((tm, 3 * d), lambda i: (i, 0)),
                   pl.BlockSpec((1, 1, N_HEADS, len_ctx, hd), kv_map),
                   pl.BlockSpec((1, 1, N_HEADS, len_ctx, hd), kv_map)],
        out_shape=[jax.ShapeDtypeStruct((t, 3 * d), BF16), kv_shape, kv_shape],
        compiler_params=_cparams(("arbitrary",)),
        name="attn_qkv",
    )(y, mod.table, mod.table, w.astype(BF16), b.reshape(1, 3 * d))


def _ctx_attn_body(qkv_ref, o_ref, *, d, scale):
    hd = d // N_HEADS
    for hh in range(N_HEADS):
        q = qkv_ref[:, hh * hd:(hh + 1) * hd]
        k = qkv_ref[:, d + hh * hd:d + (hh + 1) * hd]
        v = qkv_ref[:, 2 * d + hh * hd:2 * d + (hh + 1) * hd]
        s = _dot_nt(q, k) * scale
        e = jnp.exp(s - jnp.max(s, axis=-1, keepdims=True))
        p = (e / jnp.sum(e, axis=-1, keepdims=True)).astype(BF16)
        o_ref[:, hh * hd:(hh + 1) * hd] = _dot(p, v).astype(o_ref.dtype)


def _ctx_attn(qkv, att, *, row_block0, n_batch, len_ctx):
    t, d3 = qkv.shape
    d = d3 // 3
    scale = (d // N_HEADS) ** -0.5
    body = functools.partial(_drop_input, functools.partial(_ctx_attn_body, d=d, scale=scale), 1)
    return pl.pallas_call(
        body,
        grid=(n_batch,),
        in_specs=[pl.BlockSpec((len_ctx, d3), lambda b: (row_block0 + b, 0)),
                  pl.BlockSpec(memory_space=pl.ANY)],
        out_specs=pl.BlockSpec((len_ctx, d), lambda b: (row_block0 + b, 0)),
        out_shape=jax.ShapeDtypeStruct((t, d), BF16),
        input_output_aliases={1: 0},
        compiler_params=_cparams(("arbitrary",)),
        name="ctx_attention",
    )(qkv, att)


def _nbr_bias_body(rpb_ref, onehot_ref, band_ref, o_ref):
    picked = _dot(rpb_ref[...], onehot_ref[...], HIGHEST)
    o_ref[...] = jnp.where(band_ref[...] > 0.5, picked, -jnp.inf)


def _nbr_bias_tiles(rpb):
    n_h, n_dr, n_dc = rpb.shape
    col = np.arange(GRID_W)
    col_start = np.clip(col - WIN_COLS // 2, 0, GRID_W - WIN_COLS)
    in_band = (col[None, :] >= col_start[:, None]) & (col[None, :] < col_start[:, None] + WIN_COLS)
    dc_idx = np.clip(col[None, :] - col[:, None], -(WIN_COLS - 1), WIN_COLS - 1) + (WIN_COLS - 1)
    n_pairs = GRID_W * GRID_W
    onehot = np.zeros((LANES, n_pairs), np.float32)
    onehot[dc_idx.reshape(-1), np.arange(n_pairs)] = 1.0
    band = in_band.reshape(1, n_pairs).astype(np.float32)
    rows = n_h * n_dr
    rpb2d = jnp.pad(rpb.reshape(rows, n_dc), ((0, 0), (0, LANES - n_dc)))
    whole = lambda shape: pl.BlockSpec(shape, lambda i: (0,) * len(shape))
    cols = pl.pallas_call(
        _nbr_bias_body,
        grid=(1,),
        in_specs=[whole((rows, LANES)), whole((LANES, n_pairs)), whole((1, n_pairs))],
        out_specs=whole((rows, n_pairs)),
        out_shape=jax.ShapeDtypeStruct((rows, n_pairs), F32),
        compiler_params=_cparams(("arbitrary",)),
        name="nbr_bias",
    )(rpb2d, jnp.asarray(onehot), jnp.asarray(band))
    cols = cols.reshape(n_h, n_dr, GRID_W, GRID_W)
    n_entries = 2 * WIN_ROWS
    masked = jnp.full((n_h, n_entries - n_dr, GRID_W, GRID_W), -jnp.inf, F32)
    cols = jnp.concatenate([cols, masked], axis=1)
    zeros = jnp.zeros_like(cols)
    left = jnp.concatenate([cols, zeros], axis=-1)
    right = jnp.concatenate([zeros, cols], axis=-1)
    return jnp.concatenate([left, right], axis=1)


def _nbr_attn_body(q_ref, k_ref, v_ref, ck_ref, cv_ref, tab_ref, o_ref, *, rows, kh, scale):
    hd = ck_ref.shape[-1]
    qr = NBR_QUERY_ROWS
    span = qr + kh
    n_q = qr * GRID_W
    n_k = span * GRID_W
    masked = 2 * WIN_ROWS - 1
    for hh in range(2):
        lanes = slice(hh * hd, (hh + 1) * hd)
        kc = ck_ref[0, 0, hh].astype(BF16)
        vc = cv_ref[0, 0, hh].astype(BF16)

        def block(qb, carry):
            r0 = qb * qr
            s0 = jnp.clip(r0 - kh // 2, 0, rows - span)
            q_rows = pl.ds(pl.multiple_of(r0 * GRID_W, n_q), n_q)
            k_rows = pl.ds(pl.multiple_of(s0 * GRID_W, GRID_W), n_k)
            q = q_ref[q_rows, lanes]
            kw = k_ref[k_rows, lanes]
            vw = v_ref[k_rows, lanes]
            bias_rows = []
            for rq in range(qr):
                r = r0 + rq
                ws = jnp.clip(r - kh // 2, 0, rows - kh)
                tiles = []
                for kp in range(span // 2):
                    ke = s0 + 2 * kp
                    ko = ke + 1
                    ie = jnp.where((ke >= ws) & (ke < ws + kh), ke - r + (WIN_ROWS - 1), masked)
                    io = jnp.where((ko >= ws) & (ko < ws + kh), ko - r + (WIN_ROWS - 1), masked)
                    tiles.append(tab_ref[hh, ie] + tab_ref[hh, 2 * WIN_ROWS + io])
                bias_rows.append(jnp.concatenate(tiles, axis=1))
            bias = jnp.concatenate(bias_rows, axis=0)
            s_loc = _dot_nt(q, kw) * scale + bias
            s_ctx = _dot_nt(q, kc) * scale
            m = jnp.maximum(jnp.max(s_loc, axis=-1, keepdims=True), jnp.max(s_ctx, axis=-1, keepdims=True))
            e_loc = jnp.exp(s_loc - m)
            e_ctx = jnp.exp(s_ctx - m)
            denom = jnp.sum(e_loc, axis=-1, keepdims=True) + jnp.sum(e_ctx, axis=-1, keepdims=True)
            o = _dot(e_loc.astype(BF16), vw) + _dot(e_ctx.astype(BF16), vc)
            o_ref[q_rows, lanes] = (o / denom).astype(o_ref.dtype)
            return carry

        lax.fori_loop(0, rows // qr, block, 0)


def _nbr_attn(qkv, cache_k, cache_v, tab, layer_j, *, n_batch, len_latent):
    t, d3 = qkv.shape
    d = d3 // 3
    hd = d // N_HEADS
    rows = len_latent // GRID_W
    kh = min(WIN_ROWS, rows)
    assert rows % NBR_QUERY_ROWS == 0 and rows >= NBR_QUERY_ROWS + kh and (NBR_QUERY_ROWS + kh) % 2 == 0
    assert 2 * hd == LANES
    n_hp = N_HEADS // 2
    pc = cache_k.shape[3]
    qmap = lambda p: (lambda b, h: (b, p * n_hp + h))
    cmap = lambda b, h: (b, layer_j, h, 0, 0)
    return pl.pallas_call(
        functools.partial(_nbr_attn_body, rows=rows, kh=kh, scale=hd ** -0.5),
        grid=(n_batch, n_hp),
        in_specs=[pl.BlockSpec((len_latent, 2 * hd), qmap(0)),
                  pl.BlockSpec((len_latent, 2 * hd), qmap(1)),
                  pl.BlockSpec((len_latent, 2 * hd), qmap(2)),
                  pl.BlockSpec((1, 1, 2, pc, hd), cmap),
                  pl.BlockSpec((1, 1, 2, pc, hd), cmap),
                  pl.BlockSpec((2,) + tab.shape[1:], lambda b, h: (h, 0, 0, 0))],
        out_specs=pl.BlockSpec((len_latent, 2 * hd), lambda b, h: (b, h)),
        out_shape=jax.ShapeDtypeStruct((t, d), BF16),
        compiler_params=_cparams(("arbitrary", "arbitrary")),
        name="nbr_attention",
    )(qkv, qkv, qkv, cache_k, cache_v, tab)


def kernel(x_prompt, x_sample, cache_k, cache_v, c, c_ctx, w_ada, b_ada, ln_g, ln_b, hy_w_in, hy_b_in, hy_conv_w, hy_conv_b, hy_f_w1, hy_f_b1, hy_f_w2, hy_f_b2, hy_f_w3, hy_f_freq, hy_decay, hy_d, hy_w_out, hy_b_out, na_w_qkv, na_b_qkv, na_rpb, na_w_out, na_b_out, moe_w_group, moe_b_group, moe_w_expert, moe_b_expert, moe_w_up, moe_w_down):
    n_ctx_batch, len_ctx, d = x_prompt.shape
    n_lat_batch, len_latent, _ = x_sample.shape
    depth = w_ada.shape[0]
    alpha = (2 * depth) ** 0.25
    n_latent_rows = n_lat_batch * len_latent
    n_ctx_rows = n_ctx_batch * len_ctx
    assert len_ctx == CONV_BLOCK_CTX and len_latent % CONV_BLOCK_LATENT == 0
    nb_latent = len_latent // CONV_BLOCK_LATENT

    x = jnp.concatenate([x_sample.reshape(n_latent_rows, d), x_prompt.reshape(n_ctx_rows, d)], axis=0)

    n_cond = 16
    cond = jnp.concatenate([c, c_ctx[None, :], jnp.zeros((n_cond - n_lat_batch - 1, d), F32)], axis=0)
    mod_raw = _adaln(cond, w_ada, b_ada)
    table = mod_raw.reshape(depth, n_cond, 6, d).transpose(0, 2, 1, 3).reshape(depth * 6 * n_cond, 1, d)
    mod = _Mod(table, n_cond, len_latent, n_lat_batch)

    dft_lat = [jnp.asarray(m) for m in _dft_matrices(CONV_BLOCK_LATENT)]
    dft_ctx = [jnp.asarray(m) for m in _dft_matrices(CONV_BLOCK_CTX)]

    new_k_layers, new_v_layers = [], []
    for i in range(depth):
        j = i // 2
        if i % 2 == 0:
            x0, u = _hyena_in(x, mod, i, hy_w_in[j], hy_b_in[j], hy_conv_w[j], hy_conv_b[j],
                              n_latent_rows=n_latent_rows, len_latent=len_latent, len_ctx=len_ctx)
            filt = (hy_f_w1[j], hy_f_b1[j], hy_f_w2[j], hy_f_b2[j], hy_f_w3[j], hy_f_freq[j], hy_decay[j])
            spec_lat = _filter_spectra(nb_latent, len_latent, CONV_BLOCK_LATENT, *filt, dft_lat[0])
            spec_ctx = _filter_spectra(1, len_ctx, CONV_BLOCK_CTX, *filt, dft_ctx[0])
            tm = len_latent
            a = _hyena_conv(u, x0, spec_lat, hy_d[j], dft_lat[0], dft_lat[1], block=CONV_BLOCK_LATENT,
                            row_block0=0, n_steps=n_lat_batch, n_seq=1, nb=nb_latent, out=None)
            a = _hyena_conv(u, x0, spec_ctx, hy_d[j], dft_ctx[0], dft_ctx[1], block=CONV_BLOCK_CTX,
                            row_block0=n_lat_batch, n_steps=n_ctx_rows // tm, n_seq=tm // len_ctx, nb=1,
                            out=a)
            w_o, b_o = hy_w_out[j], hy_b_out[j]
        else:
            qkv, nk, nv = _qkv(x, mod, i, na_w_qkv[j], na_b_qkv[j], n_latent_rows=n_latent_rows,
                               n_ctx_batch=n_ctx_batch, len_ctx=len_ctx)
            new_k_layers.append(nk)
            new_v_layers.append(nv)
            tab = _nbr_bias_tiles(na_rpb[j])
            a = _nbr_attn(qkv, cache_k, cache_v, tab, j, n_batch=n_lat_batch, len_latent=len_latent)
            a = _ctx_attn(qkv, a, row_block0=n_latent_rows // len_ctx, n_batch=n_ctx_batch, len_ctx=len_ctx)
            w_o, b_o = na_w_out[j], na_b_out[j]
        x = _out_proj(a, w_o, b_o, x, mod, i, ln_g[i, 0], ln_b[i, 0], alpha)
        split = n_latent_rows if i == depth - 1 else None
        x = _moe_layer(x, mod, i, moe_w_group[i], moe_b_group[i], moe_w_expert[i], moe_b_expert[i],
                       moe_w_up[i], moe_w_down[i], ln_g[i, 1], ln_b[i, 1], alpha, split_rows=split)

    y_sample = x[0].reshape(n_lat_batch, len_latent, d)
    y_prompt = x[1].reshape(n_ctx_batch, len_ctx, d)
    new_k = jnp.concatenate(new_k_layers, axis=1)
    new_v = jnp.concatenate(new_v_layers, axis=1)
    return (y_prompt, y_sample, new_k, new_v)
```

```python
import functools
import math

import numpy as np
import jax
import jax.numpy as jnp
from jax import lax
from jax.experimental import pallas as pl
from jax.experimental.pallas import tpu as pltpu

F32 = jnp.float32
BF16 = jnp.bfloat16
HIGHEST = lax.Precision.HIGHEST

GRID_W = 64
N_BANDS = 16
N_HEADS = 16
WIN_ROWS = 8
WIN_COLS = 16
N_GROUPS = 4
EXPERTS_PER_GROUP = 4
LN_EPS = 1e-5
LOG2_E = 1.4426950408889634

LANES = 128
SUBLANES = 8
BF16_ROWS = 16
VMEM_LIMIT_BYTES = 56 * 1024 * 1024

FILTER_PAD = 128
CONV_BLOCK_CTX = 256
CONV_BLOCK_LATENT = 512

MOE_TOKEN_TILE = 512
MOE_ROW_TILE = 512
NBR_QUERY_ROWS = 4


def _cparams(sem):
    return pltpu.CompilerParams(dimension_semantics=sem, vmem_limit_bytes=VMEM_LIMIT_BYTES)


def _dot(a, b, precision=None):
    return jnp.dot(a, b, preferred_element_type=F32, precision=precision)


def _dot_nt(a, b, precision=None):
    return lax.dot_general(a, b, (((1,), (1,)), ((), ())), preferred_element_type=F32,
                           precision=precision)


def _dot3(a, b):
    a_hi = a.astype(BF16)
    a_lo = (a - a_hi.astype(F32)).astype(BF16)
    b_hi = b.astype(BF16)
    b_lo = (b - b_hi.astype(F32)).astype(BF16)
    return _dot(a_hi, b_hi) + (_dot(a_hi, b_lo) + _dot(a_lo, b_hi))


def _silu(x):
    return x / (1.0 + jnp.exp(-x))


def _layer_norm(r, g, b):
    mu = jnp.mean(r, axis=-1, keepdims=True)
    d = r - mu
    var = jnp.mean(d * d, axis=-1, keepdims=True)
    return d * lax.rsqrt(var + LN_EPS) * g + b


def _round_up(x, m):
    return (x + m - 1) // m * m


def _adaln_body(c_ref, w_ref, b_ref, o_ref):
    o_ref[0] = _dot(_silu(c_ref[...]), w_ref[0], HIGHEST) + b_ref[0]


def _adaln(cond, w_ada, b_ada):
    depth, d, d6 = w_ada.shape
    n_cond = cond.shape[0]
    tn = 1024
    return pl.pallas_call(
        _adaln_body,
        grid=(depth, d6 // tn),
        in_specs=[
            pl.BlockSpec((n_cond, d), lambda l, j: (0, 0)),
            pl.BlockSpec((1, d, tn), lambda l, j: (l, 0, j)),
            pl.BlockSpec((1, 1, tn), lambda l, j: (l, 0, j)),
        ],
        out_specs=pl.BlockSpec((1, n_cond, tn), lambda l, j: (l, 0, j)),
        out_shape=jax.ShapeDtypeStruct((depth, n_cond, d6), F32),
        compiler_params=_cparams(("arbitrary", "arbitrary")),
        name="adaln",
    )(cond, w_ada, b_ada.reshape(depth, 1, d6))


class _Mod:
    def __init__(self, table, n_cond, rows_per_cond, n_latent_cond):
        self.table = table
        self.n_cond = n_cond
        self.rows_per_cond = rows_per_cond
        self.n_latent_cond = n_latent_cond

    def spec(self, layer, which, tm, row0=0):
        base = (layer * 6 + which) * self.n_cond
        rpc, nl = self.rows_per_cond, self.n_latent_cond
        d = self.table.shape[-1]

        def index_map(i, *_):
            return (base + jnp.minimum((row0 + i * tm) // rpc, nl), 0, 0)

        return pl.BlockSpec((1, 1, d), index_map)


def _hyena_in_body(x_ref, sh_ref, sc_ref, w0_ref, w1_ref, w2_ref, b0_ref, b1_ref, b2_ref,
                   cw0_ref, cw1_ref, cw2_ref, cb0_ref, cb1_ref, cb2_ref,
                   x0_ref, u_ref, h_ref, *, seq_len):
    j = pl.program_id(1)
    tm = x_ref.shape[0]

    @pl.when(j == 0)
    def _():
        h_ref[...] = (x_ref[...] * (1.0 + sc_ref[0]) + sh_ref[0]).astype(BF16)

    pos =lax.broadcasted_iota(jnp.int32, (tm, 1), 0) & (seq_len - 1)
    first = pos == 0
    last = pos == seq_len - 1

    def part(w_ref, b_ref, cw_ref, cb_ref):
        z = _dot(h_ref[...], w_ref[...]) + b_ref[...]
        z_prev = jnp.where(first, 0.0, pltpu.roll(z, 1, 0))
        z_next = jnp.where(last, 0.0, pltpu.roll(z, tm - 1, 0))
        cw = cw_ref[...]
        return cb_ref[...] + z_prev * cw[0:1] + z * cw[1:2] + z_next * cw[2:3]

    x0_ref[...] = part(w0_ref, b0_ref, cw0_ref, cb0_ref).astype(x0_ref.dtype)
    x1 = part(w1_ref, b1_ref, cw1_ref, cb1_ref)
    v = part(w2_ref, b2_ref, cw2_ref, cb2_ref)
    u_ref[...] = (v * x1).astype(u_ref.dtype)


def _hyena_in(x, mod, layer, w_bf, b_in, conv_w, conv_b, *, seq_len, row0, tm):
    t, d = x.shape
    tn = 256
    assert t % tm == 0 and tm % seq_len == 0 and seq_len & (seq_len - 1) == 0
    nj = d // tn
    b2d = b_in.reshape(1, 3 * d)
    cb2d = conv_b.reshape(1, 3 * d)

    def col(p):
        return lambda i, j: (0, p * nj + j)

    out = jax.ShapeDtypeStruct((t, d), BF16)
    return pl.pallas_call(
        functools.partial(_hyena_in_body, seq_len=seq_len),
        grid=(t // tm, nj),
        in_specs=[pl.BlockSpec((tm, d), lambda i, j: (i, 0)),
                  mod.spec(layer, 0, tm, row0), mod.spec(layer, 1, tm, row0)]
        + [pl.BlockSpec((d, tn), col(p)) for p in range(3)]
        + [pl.BlockSpec((1, tn), col(p)) for p in range(3)]
        + [pl.BlockSpec((3, tn), col(p)) for p in range(3)]
        + [pl.BlockSpec((1, tn), col(p)) for p in range(3)],
        out_specs=[pl.BlockSpec((tm, tn), lambda i, j: (i, j))] * 2,
        out_shape=[out, out],
        scratch_shapes=[pltpu.VMEM((tm, d), BF16)],
        compiler_params=_cparams(("arbitrary", "arbitrary")),
        name=f"hyena_in_{seq_len}",
    )(x, mod.table, mod.table, w_bf, w_bf, w_bf, b2d, b2d, b2d,
      conv_w, conv_w, conv_w, cb2d, cb2d, cb2d)


def _freq_pad(block):
    return _round_up(block + 1, SUBLANES)


def _dft_matrices(block):
    n = 2 * block
    nf = block + 1
    fp = _freq_pad(block)
    f = np.arange(nf, dtype=np.float64)[:, None]
    m = np.arange(n, dtype=np.float64)[None, :]
    ang = 2.0 * np.pi * f * m / n
    fwd = np.zeros((2 * fp, n), np.float64)
    fwd[:nf] = np.cos(ang)
    fwd[fp:fp + nf] = -np.sin(ang)
    a = np.arange(block, dtype=np.float64)[:, None]
    fr = np.arange(nf, dtype=np.float64)[None, :]
    ang_i = 2.0 * np.pi * a * fr / n
    weight = np.full((1, nf), 2.0)
    weight[0, 0] = 1.0
    weight[0, nf - 1] = 1.0
    inv = np.zeros((block, 2 * fp), np.float64)
    inv[:, :nf] = weight * np.cos(ang_i) / n
    inv[:, fp:fp + nf] = -weight * np.sin(ang_i) / n
    return fwd.astype(np.float32), inv.astype(np.float32)


def _lag_tables(nb, seq_len, block):
    n = 2 * block
    lag = np.zeros((2 * nb - 1, n, 1), np.float32)
    valid = np.zeros((2 * nb - 1, n, 1), np.float32)
    m = np.arange(n)
    for dd in range(2 * nb - 1):
        delta = dd - (nb - 1)
        lg = np.where(m < block, block * delta + m, block * delta + m - n)
        ok = (m != block) & (np.abs(lg) <= seq_len - 1)
        lag[dd, :, 0] = np.where(ok, lg, 0)
        valid[dd, :, 0] = ok
    return lag, valid


def _filter_body(lag_ref, valid_ref, bands_ref, w1t_ref, w1c_ref, w1s_ref, b1_ref, w2_ref, b2_ref,
                 freq_ref, w3_ref, dec_ref, dft_ref, k_ref, *, seq_len):
    d = k_ref.shape[-1]
    lag = lag_ref[0]
    pos = jnp.abs(lag)
    tt = pos / float(max(seq_len - 1, 1))
    ang = (2.0 * math.pi / seq_len) * pos * bands_ref[...]
    freq = freq_ref[...]
    pre = (tt * w1t_ref[...] + _dot(jnp.cos(ang), w1c_ref[...], HIGHEST)
           + _dot(-jnp.sin(ang), w1s_ref[...], HIGHEST) + b1_ref[...])
    h = jnp.sin(freq * pre)
    h = jnp.sin(freq * (_dot(h, w2_ref[...], HIGHEST) + b2_ref[...]))
    hf = _dot3(h, w3_ref[:, :d]) * jnp.exp(-tt * jnp.abs(dec_ref[:, :d]))
    hb = _dot3(h, w3_ref[:, d:]) * jnp.exp(-tt * jnp.abs(dec_ref[:, d:]))
    taps = jnp.where(valid_ref[0] > 0.5, jnp.where(lag >= 0.0, hf, hb), 0.0)
    k_ref[0] = _dot3(dft_ref[...], taps)


def _filter_spectra(nb, seq_len, block, f_w1, f_b1, f_w2, f_b2, f_w3, f_freq, decay, dft_fwd):
    hid = f_w1.shape[1]
    d2 = f_w3.shape[1]
    d = d2 // 2
    n = 2 * block
    fp2 = dft_fwd.shape[0]
    lag, valid = _lag_tables(nb, seq_len, block)
    ph = FILTER_PAD - hid
    bands = np.zeros((1, LANES), np.float32)
    bands[0, :N_BANDS] = np.linspace(1e-4, N_BANDS - 1, N_BANDS, dtype=np.float32)
    w1t = jnp.pad(f_w1[0:1], ((0, 0), (0, ph)))
    w1c = jnp.pad(f_w1[1:1 + N_BANDS], ((0, LANES - N_BANDS), (0, ph)))
    w1s = jnp.pad(f_w1[1 + N_BANDS:1 + 2 * N_BANDS], ((0, LANES - N_BANDS), (0, ph)))
    b1 = jnp.pad(f_b1.reshape(1, hid), ((0, 0), (0, ph)))
    w2 = jnp.pad(f_w2, ((0, ph), (0, ph)))
    b2 = jnp.pad(f_b2.reshape(1, hid), ((0, 0), (0, ph)))
    freq = jnp.pad(f_freq.reshape(1, hid), ((0, 0), (0, ph)))
    w3 = jnp.pad(f_w3, ((0, ph), (0, 0)))
    dec = decay.reshape(1, d2)
    nd = 2 * nb - 1

    def whole(shape):
        return pl.BlockSpec(shape, lambda dd: (0,) * len(shape))

    return pl.pallas_call(
        functools.partial(_filter_body, seq_len=seq_len),
        grid=(nd,),
        in_specs=[
            pl.BlockSpec((1, n, 1), lambda dd: (dd, 0, 0)),
            pl.BlockSpec((1, n, 1), lambda dd: (dd, 0, 0)),
            whole((1, LANES)), whole((1, FILTER_PAD)), whole((LANES, FILTER_PAD)),
            whole((LANES, FILTER_PAD)), whole((1, FILTER_PAD)), whole((FILTER_PAD, FILTER_PAD)),
            whole((1, FILTER_PAD)), whole((1, FILTER_PAD)),
            whole((FILTER_PAD, d2)), whole((1, d2)), whole((fp2, n)),
        ],
        out_specs=pl.BlockSpec((1, fp2, d), lambda dd: (dd, 0, 0)),
        out_shape=jax.ShapeDtypeStruct((nd, fp2, d), F32),
        compiler_params=_cparams(("arbitrary",)),
        name=f"hyena_filter_{seq_len}",
    )(jnp.asarray(lag), jnp.asarray(valid), jnp.asarray(bands), w1t, w1c, w1s, b1, w2, b2, freq,
      w3, dec, dft_fwd)


def _hyena_conv_body(u_ref, x0_ref, k_ref, dsk_ref, dftu_ref, idft_ref, o_ref, uf_ref, yf_ref,
                     *, n_seq, nb, block):
    fp = k_ref.shape[1] // 2
    ft_rows = SUBLANES

    def fwd(blk, carry):
        rows = pl.ds(pl.multiple_of(blk * block, block), block)
        uf_ref[blk] = _dot(dftu_ref[...], u_ref[rows, :])
        return carry

    lax.fori_loop(0, n_seq * nb, fwd, 0)

    def freq_tile(ft, carry):
        r = pl.multiple_of(ft * ft_rows, ft_rows)
        re = pl.ds(r, ft_rows)
        im = pl.ds(fp + r, ft_rows)
        k_re = [k_ref[dd, re, :] for dd in range(2 * nb - 1)]
        k_im = [k_ref[dd, im, :] for dd in range(2 * nb - 1)]
        for s in range(n_seq):
            u_re = [uf_ref[s * nb + j, re, :] for j in range(nb)]
            u_im = [uf_ref[s * nb + j, im, :] for j in range(nb)]
            for i in range(nb):
                acc_re = None
                acc_im = None
                for j in range(nb):
                    dd = i - j + (nb - 1)
                    t_re = k_re[dd] * u_re[j] - k_im[dd] * u_im[j]
                    t_im = k_re[dd] * u_im[j] + k_im[dd] * u_re[j]
                    acc_re = t_re if acc_re is None else acc_re + t_re
                    acc_im = t_im if acc_im is None else acc_im + t_im
                yf_ref[s * nb + i, re, :] = acc_re
                yf_ref[s * nb + i, im, :] = acc_im
        return carry

    lax.fori_loop(0, fp // ft_rows, freq_tile, 0)

    def inv(blk, carry):
        rows = pl.ds(pl.multiple_of(blk * block, block), block)
        y = _dot(idft_ref[...], yf_ref[blk].astype(BF16))
        u_blk = u_ref[rows, :].astype(F32)
        o_ref[rows, :] = ((y + u_blk * dsk_ref[...]) * x0_ref[rows, :].astype(F32)).astype(o_ref.dtype)
        return carry

    lax.fori_loop(0, n_seq * nb, inv, 0)


def _hyena_conv(u, x0, spectra, d_skip, dft_fwd, dft_inv, *, block, n_seq, nb):
    t, d = u.shape
    ct = 256
    tm = n_seq * nb * block
    nd = 2 * nb - 1
    fp2 = dft_fwd.shape[0]
    rows = lambda c, b: (b, c)
    return pl.pallas_call(
        functools.partial(_hyena_conv_body, n_seq=n_seq, nb=nb, block=block),
        grid=(d // ct, t // tm),
        in_specs=[
            pl.BlockSpec((tm, ct), rows),
            pl.BlockSpec((tm, ct), rows),
            pl.BlockSpec((nd, fp2, ct), lambda c, b: (0, 0, c)),
            pl.BlockSpec((1, ct), lambda c, b: (0, c)),
            pl.BlockSpec((fp2, block), lambda c, b: (0, 0)),
            pl.BlockSpec((block, fp2), lambda c, b: (0, 0)),
        ],
        out_specs=pl.BlockSpec((tm, ct), rows),
        out_shape=jax.ShapeDtypeStruct((t, d), BF16),
        scratch_shapes=[pltpu.VMEM((n_seq * nb, fp2, ct), F32),
                        pltpu.VMEM((n_seq * nb, fp2, ct), F32)],
        compiler_params=_cparams(("arbitrary", "arbitrary")),
        name=f"hyena_conv_nb{nb}",
    )(u, x0, spectra, d_skip.reshape(1, d), dft_fwd[:, :block].astype(BF16), dft_inv.astype(BF16))


def _out_proj_body(*refs, alpha, split_tile, x_parts):
    a_refs = refs[0:2]
    x_refs = refs[2:2 + x_parts]
    w_ref, b_ref, gate_ref, g_ref, beta_ref, o_ref = refs[2 + x_parts:]
    i = pl.program_id(0)

    def run(a_ref, x_ref):
        o = _dot(a_ref[...], w_ref[...]) + b_ref[...]
        r = alpha * x_ref[...] + gate_ref[0] * o
        o_ref[...] = _layer_norm(r, g_ref[...], beta_ref[...])

    @pl.when(i < split_tile)
    def _():
        run(a_refs[0], x_refs[0])

    @pl.when(i >= split_tile)
    def _():
        run(a_refs[1], x_refs[-1])


def _out_proj(a_parts, x_parts, w, b, mod, layer, ln_g, ln_b, alpha):
    d = a_parts[0].shape[1]
    tm = 512
    split_tile = a_parts[0].shape[0] // tm
    t = a_parts[0].shape[0] + a_parts[1].shape[0]
    first = lambda i: (jnp.minimum(i, split_tile - 1), 0)
    second = lambda i: (jnp.maximum(i - split_tile, 0), 0)
    row = lambda i: (i, 0)
    const = lambda i: (0, 0)
    part_specs = [pl.BlockSpec((tm, d), first), pl.BlockSpec((tm, d), second)]
    x_specs = part_specs if len(x_parts) == 2 else [pl.BlockSpec((tm, d), row)]
    return pl.pallas_call(
        functools.partial(_out_proj_body, alpha=alpha, split_tile=split_tile, x_parts=len(x_parts)),
        grid=(t // tm,),
        in_specs=part_specs + x_specs
        + [pl.BlockSpec((d, d), const), pl.BlockSpec((1, d), const), mod.spec(layer, 2, tm),
           pl.BlockSpec((1, d), const), pl.BlockSpec((1, d), const)],
        out_specs=pl.BlockSpec((tm, d), row),
        out_shape=jax.ShapeDtypeStruct((t, d), F32),
        compiler_params=_cparams(("arbitrary",)),
        name="out_proj_ln",
    )(*a_parts, *x_parts, w.astype(BF16), b.reshape(1, d), mod.table,
      ln_g.reshape(1, d), ln_b.reshape(1, d))


def _router_body(y_ref, sh_ref, sc_ref, w_ref, b_ref, t_ref, route_ref, cnt_ref):
    t = y_ref[...] * (1.0 + sc_ref[0]) + sh_ref[0]
    t_ref[...] = t.astype(BF16)
    lg = _dot3(t, w_ref[...]) + b_ref[...]
    lane = lax.broadcasted_iota(jnp.int32, lg.shape, 1)
    neg = -jnp.inf
    far = jnp.int32(LANES)
    g_mask = lane < N_GROUPS
    gl = jnp.where(g_mask, lg, neg)
    g_max = jnp.max(gl, axis=-1, keepdims=True)
    g_sel = jnp.min(jnp.where(gl == g_max, lane, far), axis=-1, keepdims=True)
    g_w = 1.0 / jnp.sum(jnp.where(g_mask, jnp.exp(lg - g_max), 0.0), axis=-1, keepdims=True)
    e_lo = N_GROUPS + EXPERTS_PER_GROUP * g_sel
    e_mask = (lane >= e_lo) & (lane < e_lo + EXPERTS_PER_GROUP)
    el = jnp.where(e_mask, lg, neg)
    m1 = jnp.max(el, axis=-1, keepdims=True)
    i1 = jnp.min(jnp.where(el == m1, lane, far), axis=-1, keepdims=True)
    el2 = jnp.where(lane == i1, neg, el)
    m2 = jnp.max(el2, axis=-1, keepdims=True)
    i2 = jnp.min(jnp.where(el2 == m2, lane, far), axis=-1, keepdims=True)
    ratio = jnp.exp(m2 - m1)
    w1 = 1.0 / (1.0 + ratio)
    w2 = ratio / (1.0 + ratio)
    e1 = i1 - N_GROUPS
    e2 = i2 - N_GROUPS
    route_ref[...] = jnp.where(lane == 0, e1.astype(F32),
                               jnp.where(lane == 1, e2.astype(F32),
                                         jnp.where(lane == 2, g_w * w1,
                                                   jnp.where(lane == 3, g_w * w2, 0.0))))
    hit = ((lane == e1) | (lane == e2)).astype(F32)
    cnt_ref[0] = jnp.broadcast_to(jnp.sum(hit, axis=0, keepdims=True), cnt_ref.shape[1:])


def _router(y, mod, layer, w_group, b_group, w_expert, b_expert):
    t, d = y.shape
    tm = MOE_TOKEN_TILE
    n_e = w_expert.shape[1]
    pad = LANES - N_GROUPS - n_e
    w = jnp.pad(jnp.concatenate([w_group, w_expert], axis=1), ((0, 0), (0, pad)))
    b = jnp.pad(jnp.concatenate([b_group, b_expert]), (0, pad)).reshape(1, LANES)
    row = lambda i: (i, 0)
    const = lambda i: (0, 0)
    return pl.pallas_call(
        _router_body,
        grid=(t // tm,),
        in_specs=[pl.BlockSpec((tm, d), row), mod.spec(layer, 3, tm), mod.spec(layer, 4, tm),
                  pl.BlockSpec((d, LANES), const), pl.BlockSpec((1, LANES), const)],
        out_specs=[pl.BlockSpec((tm, d), row), pl.BlockSpec((tm, LANES), row),
                   pl.BlockSpec((1, SUBLANES, LANES), lambda i: (i, 0, 0))],
        out_shape=[jax.ShapeDtypeStruct((t, d), BF16), jax.ShapeDtypeStruct((t, LANES), F32),
                   jax.ShapeDtypeStruct((t // tm, SUBLANES, LANES), F32)],
        compiler_params=_cparams(("arbitrary",)),
        name="moe_router",
    )(y, mod.table, mod.table, w, b)


def _moe_layout(counts, n_experts, n_row_tiles_max):
    cnt = counts[:, 0, :n_experts].astype(jnp.int32)
    run_len = (cnt + (BF16_ROWS - 1)) // BF16_ROWS * BF16_ROWS
    total = jnp.sum(run_len, axis=0)
    region = (total + (MOE_ROW_TILE - 1)) // MOE_ROW_TILE * MOE_ROW_TILE
    region_end = jnp.cumsum(region)
    expert_start = region_end - region
    run_start = expert_start[None, :] + jnp.cumsum(run_len, axis=0) - run_len
    n_used = (region_end[-1] // MOE_ROW_TILE).reshape(1).astype(jnp.int32)
    tile_idx = jnp.arange(n_row_tiles_max, dtype=jnp.int32)
    tile_expert = jnp.minimum(
        jnp.sum((tile_idx[:, None] >= (region_end // MOE_ROW_TILE)[None, :]).astype(jnp.int32), axis=1),
        n_experts - 1).astype(jnp.int32)
    used_rows = region_end[-1:]
    tail_chunks = (n_row_tiles_max * MOE_ROW_TILE - used_rows) // (MOE_ROW_TILE // 2)
    return dict(run_start=run_start.reshape(-1).astype(jnp.int32),
                run_len=run_len.reshape(-1).astype(jnp.int32),
                gap_start=jnp.concatenate([expert_start + total, used_rows]).astype(jnp.int32),
                gap_len=jnp.concatenate([region - total, tail_chunks]).astype(jnp.int32),
                tile_expert=tile_expert, n_used=n_used)


def _local_positions(route, run_len_ref, tile, n_experts):
    tm = route.shape[0]
    lane = lax.broadcasted_iota(jnp.int32, (tm, LANES), 1)
    e1 = route[:, 0:1].astype(jnp.int32)
    e2 = route[:, 1:2].astype(jnp.int32)
    hit = ((lane == e1) | (lane == e2)).astype(BF16)
    earlier = (lax.broadcasted_iota(jnp.int32, (tm, tm), 1)
               < lax.broadcasted_iota(jnp.int32, (tm, tm), 0)).astype(BF16)
    rank = _dot(earlier, hit)
    lane1 = lax.broadcasted_iota(jnp.int32, (1, LANES), 1)
    starts = jnp.zeros((1, LANES), F32)
    off = jnp.int32(0)
    for e in range(n_experts):
        starts = jnp.where(lane1 == e, off.astype(F32), starts)
        off = off + run_len_ref[tile * n_experts + e]
    pos = rank + starts
    p1 = jnp.sum(jnp.where(lane == e1, pos, 0.0), axis=-1, keepdims=True)
    p2 = jnp.sum(jnp.where(lane == e2, pos, 0.0), axis=-1, keepdims=True)
    return p1, p2


def _chunk_copies(src_ref, src_off, dst_ref, dst_off, n, sem, max_chunk, advance_src=True):
    out = []
    off = jnp.int32(0)
    bit = max_chunk
    while bit >= BF16_ROWS:
        take = n & bit
        s = pl.multiple_of(src_off + off, BF16_ROWS) if advance_src else src_off
        t = pl.multiple_of(dst_off + off, BF16_ROWS)
        cp = pltpu.make_async_copy(src_ref.at[pl.ds(s, bit)], dst_ref.at[pl.ds(t, bit)], sem)
        out.append((take != 0, cp))
        off = off + take
        bit //= 2
    return out


def _start_all(copies):
    for cond, cp in copies:
        pl.when(cond)(cp.start)


def _wait_all(copies):
    for cond, cp in copies:
        pl.when(cond)(cp.wait)


def _dispatch_body(rs_ref, rl_ref, gs_ref, gl_ref, t_ref, route_ref, xg_ref, sorted_ref, zero_ref, sem,
                   *, n_experts):
    tile = pl.program_id(0)
    tm = t_ref.shape[0]
    n_rows = sorted_ref.shape[0]
    route = route_ref[...]
    p1, p2 = _local_positions(route, rl_ref, tile, n_experts)
    lane = lax.broadcasted_iota(jnp.int32, (tm, LANES), 1)
    both = jnp.where(lane == 0, p1, jnp.where(lane == 1, p2, 0.0))
    pick = (lax.broadcasted_iota(jnp.int32, (SUBLANES, LANES), 0)
            == lax.broadcasted_iota(jnp.int32, (SUBLANES, LANES), 1)).astype(F32)
    as_rows = _dot_nt(pick, both, HIGHEST)
    r_iota = lax.broadcasted_iota(jnp.int32, (n_rows, tm), 0).astype(F32)
    select = ((r_iota == as_rows[0:1]) | (r_iota == as_rows[1:2])).astype(BF16)
    sorted_ref[...] = _dot(select, t_ref[...]).astype(BF16)

    copies = []
    off = jnp.int32(0)
    for e in range(n_experts):
        n = rl_ref[tile * n_experts + e]
        copies += _chunk_copies(sorted_ref, off, xg_ref, rs_ref[tile * n_experts + e], n, sem, tm)
        off = off + n
    _start_all(copies)
    _wait_all(copies)

    @pl.when(tile == pl.num_programs(0) - 1)
    def _():
        zero_ref[...] = jnp.zeros_like(zero_ref)
        fills = []
        for e in range(n_experts):
            fills += _chunk_copies(zero_ref, 0, xg_ref, gs_ref[e], gl_ref[e], sem, zero_ref.shape[0],
                                   advance_src=False)
        _start_all(fills)
        _wait_all(fills)
        z_rows = zero_ref.shape[0]
        tail0 = gs_ref[n_experts]

        def tail_copy(c):
            dst = pl.multiple_of(tail0 + c * z_rows, z_rows)
            return pltpu.make_async_copy(zero_ref, xg_ref.at[pl.ds(dst, z_rows)], sem)

        def start(c, carry):
            tail_copy(c).start()
            return carry

        def wait(c, carry):
            tail_copy(c).wait()
            return carry

        lax.fori_loop(0, gl_ref[n_experts], start, 0)
        lax.fori_loop(0, gl_ref[n_experts], wait, 0)


def _sorted_rows(tm, n_experts):
    return _round_up(2 * tm + n_experts * (BF16_ROWS - 1), LANES)


def _dispatch(tb, route, layout, n_experts, n_rows_max):
    t, d = tb.shape
    tm = MOE_TOKEN_TILE
    row = lambda i, *_: (i, 0)
    grid_spec = pltpu.PrefetchScalarGridSpec(
        num_scalar_prefetch=4,
        grid=(t // tm,),
        in_specs=[pl.BlockSpec((tm, d), row), pl.BlockSpec((tm, LANES), row)],
        out_specs=pl.BlockSpec(memory_space=pl.ANY),
        scratch_shapes=[pltpu.VMEM((_sorted_rows(tm, n_experts), d), BF16),
                        pltpu.VMEM((MOE_ROW_TILE // 2, d), BF16),
                        pltpu.SemaphoreType.DMA(())],
    )
    return pl.pallas_call(
        functools.partial(_dispatch_body, n_experts=n_experts),
        grid_spec=grid_spec,
        out_shape=jax.ShapeDtypeStruct((n_rows_max, d), BF16),
        compiler_params=_cparams(("arbitrary",)),
        name="moe_dispatch",
    )(layout["run_start"], layout["run_len"], layout["gap_start"], layout["gap_len"], tb, route)


def _experts_body(te_ref, nu_ref, x_ref, wu_ref, wd_ref, y_ref, wub_ref, wdb_ref, *, d_expert):
    g = pl.program_id(0)

    @pl.when(g < nu_ref[0])
    def _():
        prev = te_ref[jnp.maximum(g - 1, 0)]

        @pl.when((g == 0) | (te_ref[g] != prev))
        def _():
            wub_ref[...] = wu_ref[0, 0].astype(BF16)
            wdb_ref[...] = wd_ref[0, 0].astype(BF16)

        ab = _dot(x_ref[...], wub_ref[...])
        h = _silu(ab[:, :d_expert]) * ab[:, d_expert:]
        y_ref[...] = _dot(h.astype(BF16), wdb_ref[...]).astype(y_ref.dtype)

    @pl.when(g >= nu_ref[0])
    def _():
        y_ref[...] = jnp.zeros_like(y_ref)


def _experts(xg, w_up, w_down, layer, layout):
    n_rows, d = xg.shape
    d_up = w_up.shape[-1]
    d_expert = d_up // 2
    tm = MOE_ROW_TILE
    used = lambda g, nu: jnp.maximum(jnp.minimum(g, nu[0] - 1), 0)
    grid_spec = pltpu.PrefetchScalarGridSpec(
        num_scalar_prefetch=2,
        grid=(n_rows // tm,),
        in_specs=[pl.BlockSpec((tm, d), lambda g, te, nu: (used(g, nu), 0)),
                  pl.BlockSpec((1, 1, d, d_up), lambda g, te, nu: (layer, te[used(g, nu)], 0, 0)),
                  pl.BlockSpec((1, 1, d_expert, d), lambda g, te, nu: (layer, te[used(g, nu)], 0, 0))],
        out_specs=pl.BlockSpec((tm, d), lambda g, te, nu: (g, 0)),
        scratch_shapes=[pltpu.VMEM((d, d_up), BF16), pltpu.VMEM((d_expert, d), BF16)],
    )
    return pl.pallas_call(
        functools.partial(_experts_body, d_expert=d_expert),
        grid_spec=grid_spec,
        out_shape=jax.ShapeDtypeStruct((n_rows, d), BF16),
        compiler_params=_cparams(("arbitrary",)),
        name="moe_experts",
    )(layout["tile_expert"], layout["n_used"], xg, w_up, w_down)


def _combine_body(rs_ref, rl_ref, yg_ref, route_ref, x_ref, gate_ref, g_ref, beta_ref, *rest,
                  alpha, n_experts, split_tile):
    if split_tile is None:
        o_ref, ybuf_ref, sem = rest
    else:
        o_ref, o2_ref, ybuf_ref, sem = rest
    tile = pl.program_id(0)
    tm = x_ref.shape[0]
    n_rows = ybuf_ref.shape[0]
    ybuf_ref[2 * tm:, :] = jnp.zeros((n_rows - 2 * tm, ybuf_ref.shape[1]), ybuf_ref.dtype)

    copies = []
    off = jnp.int32(0)
    for e in range(n_experts):
        n = rl_ref[tile * n_experts + e]
        copies += _chunk_copies(yg_ref, rs_ref[tile * n_experts + e], ybuf_ref, off, n, sem, tm)
        off = off + n
    _start_all(copies)

    route = route_ref[...]
    p1, p2 = _local_positions(route, rl_ref, tile, n_experts)
    r_lane = lax.broadcasted_iota(jnp.int32, (tm, n_rows), 1).astype(F32)
    cmat = (jnp.where(r_lane == p1, route[:, 2:3], 0.0)
            + jnp.where(r_lane == p2, route[:, 3:4], 0.0)).astype(BF16)
    _wait_all(copies)
    y = _dot(cmat, ybuf_ref[...])
    r = alpha * x_ref[...] + gate_ref[0] * y
    res = _layer_norm(r, g_ref[...], beta_ref[...])
    if split_tile is None:
        o_ref[...] = res
    else:
        @pl.when(tile < split_tile)
        def _():
            o_ref[...] = res

        @pl.when(tile >= split_tile)
        def _():
            o2_ref[...] = res


def _combine(yg, route, x, layout, mod, layer, ln_g, ln_b, alpha, n_experts, split_rows=None):
    t, d = x.shape
    tm = MOE_TOKEN_TILE
    row = lambda i, *_: (i, 0)
    const = lambda i, *_: (0, 0)
    if split_rows is None:
        split_tile = None
        out_specs = pl.BlockSpec((tm, d), row)
        out_shape = jax.ShapeDtypeStruct((t, d), F32)
    else:
        assert split_rows % tm == 0
        split_tile = split_rows // tm
        out_specs = [pl.BlockSpec((tm, d), lambda i, *_: (jnp.minimum(i, split_tile - 1), 0)),
                     pl.BlockSpec((tm, d), lambda i, *_: (jnp.maximum(i - split_tile, 0), 0))]
        out_shape = [jax.ShapeDtypeStruct((split_rows, d), F32),
                     jax.ShapeDtypeStruct((t - split_rows, d), F32)]
    grid_spec = pltpu.PrefetchScalarGridSpec(
        num_scalar_prefetch=2,
        grid=(t // tm,),
        in_specs=[pl.BlockSpec(memory_space=pl.ANY), pl.BlockSpec((tm, LANES), row),
                  pl.BlockSpec((tm, d), row), mod.spec(layer, 5, tm),
                  pl.BlockSpec((1, d), const), pl.BlockSpec((1, d), const)],
        out_specs=out_specs,
        scratch_shapes=[pltpu.VMEM((_sorted_rows(tm, n_experts), d), BF16),
                        pltpu.SemaphoreType.DMA(())],
    )
    return pl.pallas_call(
        functools.partial(_combine_body, alpha=alpha, n_experts=n_experts, split_tile=split_tile),
        grid_spec=grid_spec,
        out_shape=out_shape,
        compiler_params=_cparams(("arbitrary",)),
        name="moe_combine_ln",
    )(layout["run_start"], layout["run_len"], yg, route, x, mod.table,
      ln_g.reshape(1, d), ln_b.reshape(1, d))


def _moe_layer(x, mod, layer, w_group, b_group, w_expert, b_expert, w_up, w_down, ln_g, ln_b, alpha,
               split_rows=None):
    t, d = x.shape
    n_e = w_up.shape[1]
    n_tok_tiles = t // MOE_TOKEN_TILE
    max_rows = 2 * t + n_tok_tiles * n_e * (BF16_ROWS - 1) + n_e * (MOE_ROW_TILE - BF16_ROWS)
    n_rows_max = _round_up(max_rows, MOE_ROW_TILE)
    tb, route, counts = _router(x, mod, layer, w_group, b_group, w_expert, b_expert)
    layout = _moe_layout(counts, n_e, n_rows_max // MOE_ROW_TILE)
    xg = _dispatch(tb, route, layout, n_e, n_rows_max)
    yg = _experts(xg, w_up, w_down, layer, layout)
    return _combine(yg, route, x, layout, mod, layer, ln_g, ln_b, alpha, n_e, split_rows)


def _qkv_body(y_ref, sh_ref, sc_ref, w_ref, b_ref, qkv_ref, nk_ref, nv_ref, *, n_latent_tiles, d):
    i = pl.program_id(0)
    h = (y_ref[...] * (1.0 + sc_ref[0]) + sh_ref[0]).astype(BF16)
    z = _dot(h, w_ref[...]) + b_ref[...]
    qkv_ref[...] = z.astype(BF16)

    @pl.when(i >= n_latent_tiles)
    def _():
        hd = nk_ref.shape[-1]
        for hh in range(nk_ref.shape[2]):
            nk_ref[0, 0, hh] = z[:, d + hh * hd:d + (hh + 1) * hd]
            nv_ref[0, 0, hh] = z[:, 2 * d + hh * hd:2 * d + (hh + 1) * hd]


def _qkv(y, mod, layer, w, b, *, n_latent_rows, n_ctx_batch, len_ctx):
    t, d = y.shape
    tm = len_ctx
    hd = d // N_HEADS
    n_lat = n_latent_rows // tm
    kv_map = lambda i: (jnp.maximum(i - n_lat, 0), 0, 0, 0, 0)
    kv_shape = jax.ShapeDtypeStruct((n_ctx_batch, 1, N_HEADS, len_ctx, hd), F32)
    return pl.pallas_call(
        functools.partial(_qkv_body, n_latent_tiles=n_lat, d=d),
        grid=(t // tm,),
        in_specs=[pl.BlockSpec((tm, d), lambda i: (i, 0)), mod.spec(layer, 0, tm), mod.spec(layer, 1, tm),
                  pl.BlockSpec((d, 3 * d), lambda i: (0, 0)), pl.BlockSpec((1, 3 * d), lambda i: (0, 0))],
        out_specs=[pl.BlockSpec((tm, 3 * d), lambda i: (i, 0)),
                   pl.BlockSpec((1, 1, N_HEADS, len_ctx, hd), kv_map),
                   pl.BlockSpec((1, 1, N_HEADS, len_ctx, hd), kv_map)],
        out_shape=[jax.ShapeDtypeStruct((t, 3 * d), BF16), kv_shape, kv_shape],
        compiler_params=_cparams(("arbitrary",)),
        name="attn_qkv",
    )(y, mod.table, mod.table, w.astype(BF16), b.reshape(1, 3 * d))


def _ctx_attn_body(qkv_ref, o_ref, *, d, scale):
    pair = 2 * (d // N_HEADS)
    left = lax.broadcasted_iota(jnp.int32, (1, pair), 1) < pair // 2
    c = scale * LOG2_E
    for hp in range(d // pair):
        cols = slice(hp * pair, (hp + 1) * pair)
        q2 = qkv_ref[:, cols]
        k2 = qkv_ref[:, d + hp * pair:d + (hp + 1) * pair]
        v2 = qkv_ref[:, 2 * d + hp * pair:2 * d + (hp + 1) * pair]
        outs = []
        for mask in (left, ~left):
            s = _dot_nt(jnp.where(mask, q2, jnp.zeros_like(q2)), k2) * c
            e = jnp.exp2(s - jnp.max(s, axis=-1, keepdims=True))
            p = (e / jnp.sum(e, axis=-1, keepdims=True)).astype(BF16)
            outs.append(_dot(p, v2))
        o_ref[:, cols] = jnp.where(left, outs[0], outs[1]).astype(o_ref.dtype)


def _ctx_attn(qkv, *, row_block0, n_batch, len_ctx):
    d3 = qkv.shape[1]
    d = d3 // 3
    scale = (d // N_HEADS) ** -0.5
    return pl.pallas_call(
        functools.partial(_ctx_attn_body, d=d, scale=scale),
        grid=(n_batch,),
        in_specs=[pl.BlockSpec((len_ctx, d3), lambda b: (row_block0 + b, 0))],
        out_specs=pl.BlockSpec((len_ctx, d), lambda b: (b, 0)),
        out_shape=jax.ShapeDtypeStruct((n_batch * len_ctx, d), BF16),
        compiler_params=_cparams(("arbitrary",)),
        name="ctx_attention",
    )(qkv)


def _nbr_bias_body(rpb_ref, onehot_ref, band_ref, o_ref):
    picked = _dot(rpb_ref[...], onehot_ref[...], HIGHEST)
    o_ref[...] = jnp.where(band_ref[...] > 0.5, picked * LOG2_E, -jnp.inf)


def _nbr_bias_tiles(rpb):
    n_h, n_dr, n_dc = rpb.shape
    col = np.arange(GRID_W)
    col_start = np.clip(col - WIN_COLS // 2, 0, GRID_W - WIN_COLS)
    in_band = (col[None, :] >= col_start[:, None]) & (col[None, :] < col_start[:, None] + WIN_COLS)
    dc_idx = np.clip(col[None, :] - col[:, None], -(WIN_COLS - 1), WIN_COLS - 1) + (WIN_COLS - 1)
    n_pairs = GRID_W * GRID_W
    onehot = np.zeros((LANES, n_pairs), np.float32)
    onehot[dc_idx.reshape(-1), np.arange(n_pairs)] = 1.0
    band = in_band.reshape(1, n_pairs).astype(np.float32)
    rows = n_h * n_dr
    rpb2d = jnp.pad(rpb.reshape(rows, n_dc), ((0, 0), (0, LANES - n_dc)))
    whole = lambda shape: pl.BlockSpec(shape, lambda i: (0,) * len(shape))
    cols = pl.pallas_call(
        _nbr_bias_body,
        grid=(1,),
        in_specs=[whole((rows, LANES)), whole((LANES, n_pairs)), whole((1, n_pairs))],
        out_specs=whole((rows, n_pairs)),
        out_shape=jax.ShapeDtypeStruct((rows, n_pairs), F32),
        compiler_params=_cparams(("arbitrary",)),
        name="nbr_bias",
    )(rpb2d, jnp.asarray(onehot), jnp.asarray(band))
    cols = cols.reshape(n_h, n_dr, GRID_W, GRID_W)
    n_entries = 2 * WIN_ROWS
    masked = jnp.full((n_h, n_entries - n_dr, GRID_W, GRID_W), -jnp.inf, F32)
    cols = jnp.concatenate([cols, masked], axis=1)
    zeros = jnp.zeros_like(cols)
    left = jnp.concatenate([cols, zeros], axis=-1)
    right = jnp.concatenate([zeros, cols], axis=-1)
    return jnp.concatenate([left, right], axis=1)


def _nbr_attn_body(q_ref, k_ref, v_ref, ck_ref, cv_ref, tab_ref, o_ref, *, rows, kh, scale):
    hd = ck_ref.shape[-1]
    qr = NBR_QUERY_ROWS
    span = qr + kh
    n_q = qr * GRID_W
    n_k = span * GRID_W
    masked = 2 * WIN_ROWS - 1
    c = scale * LOG2_E
    left = lax.broadcasted_iota(jnp.int32, (1, 2 * hd), 1) < hd
    kc2 = jnp.concatenate([ck_ref[0, 0, 0], ck_ref[0, 0, 1]], axis=1).astype(BF16)
    vc2 = jnp.concatenate([cv_ref[0, 0, 0], cv_ref[0, 0, 1]], axis=1).astype(BF16)

    def block(qb, carry):
        r0 = qb * qr
        s0 = jnp.clip(r0 - kh // 2, 0, rows - span)
        q_rows = pl.ds(pl.multiple_of(r0 * GRID_W, n_q), n_q)
        k_rows = pl.ds(pl.multiple_of(s0 * GRID_W, GRID_W), n_k)
        q2 = q_ref[q_rows, :]
        kw2 = k_ref[k_rows, :]
        vw2 = v_ref[k_rows, :]
        outs = []
        for hh, mask in enumerate((left, ~left)):
            bias_rows = []
            for rq in range(qr):
                r = r0 + rq
                ws = jnp.clip(r - kh // 2, 0, rows - kh)
                tiles = []
                for kp in range(span // 2):
                    ke = s0 + 2 * kp
                    ko = ke + 1
                    ie = jnp.where((ke >= ws) & (ke < ws + kh), ke - r + (WIN_ROWS - 1), masked)
                    io = jnp.where((ko >= ws) & (ko < ws + kh), ko - r + (WIN_ROWS - 1), masked)
                    tiles.append(tab_ref[hh, ie] + tab_ref[hh, 2 * WIN_ROWS + io])
                bias_rows.append(jnp.concatenate(tiles, axis=1))
            bias = jnp.concatenate(bias_rows, axis=0)
            qh = jnp.where(mask, q2, jnp.zeros_like(q2))
            s_loc = _dot_nt(qh, kw2) * c + bias
            s_ctx = _dot_nt(qh, kc2) * c
            m = jnp.maximum(jnp.max(s_loc, axis=-1, keepdims=True), jnp.max(s_ctx, axis=-1, keepdims=True))
            e_loc = jnp.exp2(s_loc - m)
            e_ctx = jnp.exp2(s_ctx - m)
            denom = jnp.sum(e_loc, axis=-1, keepdims=True) + jnp.sum(e_ctx, axis=-1, keepdims=True)
            outs.append((_dot(e_loc.astype(BF16), vw2) + _dot(e_ctx.astype(BF16), vc2)) / denom)
        o_ref[q_rows, :] = jnp.where(left, outs[0], outs[1]).astype(o_ref.dtype)
        return carry

    lax.fori_loop(0, rows // qr, block, 0)


def _nbr_attn(qkv, cache_k, cache_v, tab, layer_j, *, n_batch, len_latent):
    d3 = qkv.shape[1]
    t = n_batch * len_latent
    d = d3 // 3
    hd = d // N_HEADS
    rows = len_latent // GRID_W
    kh = min(WIN_ROWS, rows)
    assert rows % NBR_QUERY_ROWS == 0 and rows >= NBR_QUERY_ROWS + kh and (NBR_QUERY_ROWS + kh) % 2 == 0
    assert 2 * hd == LANES
    n_hp = N_HEADS // 2
    pc = cache_k.shape[3]
    qmap = lambda p: (lambda b, h: (b, p * n_hp + h))
    cmap = lambda b, h: (b, layer_j, h, 0, 0)
    return pl.pallas_call(
        functools.partial(_nbr_attn_body, rows=rows, kh=kh, scale=hd ** -0.5),
        grid=(n_batch, n_hp),
        in_specs=[pl.BlockSpec((len_latent, 2 * hd), qmap(0)),
                  pl.BlockSpec((len_latent, 2 * hd), qmap(1)),
                  pl.BlockSpec((len_latent, 2 * hd), qmap(2)),
                  pl.BlockSpec((1, 1, 2, pc, hd), cmap),
                  pl.BlockSpec((1, 1, 2, pc, hd), cmap),
                  pl.BlockSpec((2,) + tab.shape[1:], lambda b, h: (h, 0, 0, 0))],
        out_specs=pl.BlockSpec((len_latent, 2 * hd), lambda b, h: (b, h)),
        out_shape=jax.ShapeDtypeStruct((t, d), BF16),
        compiler_params=_cparams(("arbitrary", "arbitrary")),
        name="nbr_attention",
    )(qkv, qkv, qkv, cache_k, cache_v, tab)


def kernel(x_prompt, x_sample, cache_k, cache_v, c, c_ctx, w_ada, b_ada, ln_g, ln_b, hy_w_in, hy_b_in, hy_conv_w, hy_conv_b, hy_f_w1, hy_f_b1, hy_f_w2, hy_f_b2, hy_f_w3, hy_f_freq, hy_decay, hy_d, hy_w_out, hy_b_out, na_w_qkv, na_b_qkv, na_rpb, na_w_out, na_b_out, moe_w_group, moe_b_group, moe_w_expert, moe_b_expert, moe_w_up, moe_w_down):
    n_ctx_batch, len_ctx, d = x_prompt.shape
    n_lat_batch, len_latent, _ = x_sample.shape
    depth = w_ada.shape[0]
    alpha = (2 * depth) ** 0.25
    n_latent_rows = n_lat_batch * len_latent
    n_ctx_rows = n_ctx_batch * len_ctx
    assert len_ctx == CONV_BLOCK_CTX and len_latent % CONV_BLOCK_LATENT == 0
    nb_latent = len_latent // CONV_BLOCK_LATENT

    x_parts = [x_sample.reshape(n_latent_rows, d), x_prompt.reshape(n_ctx_rows, d)]

    n_cond = 16
    cond = jnp.concatenate([c, c_ctx[None, :], jnp.zeros((n_cond - n_lat_batch - 1, d), F32)], axis=0)
    mod_raw = _adaln(cond, w_ada, b_ada)
    table = mod_raw.reshape(depth, n_cond, 6, d).transpose(0, 2, 1, 3).reshape(depth * 6 * n_cond, 1, d)
    mod = _Mod(table, n_cond, len_latent, n_lat_batch)

    dft_lat = [jnp.asarray(m) for m in _dft_matrices(CONV_BLOCK_LATENT)]
    dft_ctx = [jnp.asarray(m) for m in _dft_matrices(CONV_BLOCK_CTX)]

    new_k_layers, new_v_layers = [], []
    for i in range(depth):
        j = i // 2
        if i % 2 == 0:
            if len(x_parts) == 1:
                x_parts = [x_parts[0][:n_latent_rows], x_parts[0][n_latent_rows:]]
            tm = len_latent
            w_bf = hy_w_in[j].astype(BF16)
            conv = (hy_b_in[j], hy_conv_w[j], hy_conv_b[j])
            x0_l, u_l = _hyena_in(x_parts[0], mod, i, w_bf, *conv, seq_len=len_latent, row0=0, tm=tm)
            x0_c, u_c = _hyena_in(x_parts[1], mod, i, w_bf, *conv, seq_len=len_ctx, row0=n_latent_rows, tm=tm)
            filt = (hy_f_w1[j], hy_f_b1[j], hy_f_w2[j], hy_f_b2[j], hy_f_w3[j], hy_f_freq[j], hy_decay[j])
            spec_lat = _filter_spectra(nb_latent, len_latent, CONV_BLOCK_LATENT, *filt, dft_lat[0])
            spec_ctx = _filter_spectra(1, len_ctx, CONV_BLOCK_CTX, *filt, dft_ctx[0])
            a_parts = [
                _hyena_conv(u_l, x0_l, spec_lat, hy_d[j], dft_lat[0], dft_lat[1], block=CONV_BLOCK_LATENT,
                            n_seq=1, nb=nb_latent),
                _hyena_conv(u_c, x0_c, spec_ctx, hy_d[j], dft_ctx[0], dft_ctx[1], block=CONV_BLOCK_CTX,
                            n_seq=tm // len_ctx, nb=1),
            ]
            w_o, b_o = hy_w_out[j], hy_b_out[j]
        else:
            if len(x_parts) == 2:
                x_parts = [jnp.concatenate(x_parts, axis=0)]
            qkv, nk, nv = _qkv(x_parts[0], mod, i, na_w_qkv[j], na_b_qkv[j], n_latent_rows=n_latent_rows,
                               n_ctx_batch=n_ctx_batch, len_ctx=len_ctx)
            new_k_layers.append(nk)
            new_v_layers.append(nv)
            tab = _nbr_bias_tiles(na_rpb[j])
            a_parts = [
                _nbr_attn(qkv, cache_k, cache_v, tab, j, n_batch=n_lat_batch, len_latent=len_latent),
                _ctx_attn(qkv, row_block0=n_latent_rows // len_ctx, n_batch=n_ctx_batch, len_ctx=len_ctx),
            ]
            w_o, b_o = na_w_out[j], na_b_out[j]
        x = _out_proj(a_parts, x_parts, w_o, b_o, mod, i, ln_g[i, 0], ln_b[i, 0], alpha)
        split = n_latent_rows if i == depth - 1 else None
        x = _moe_layer(x, mod, i, moe_w_group[i], moe_b_group[i], moe_w_expert[i], moe_b_expert[i],
                       moe_w_up, moe_w_down, ln_g[i, 1], ln_b[i, 1], alpha, split_rows=split)
        x_parts = list(x) if split is not None else [x]

    y_sample = x_parts[0].reshape(n_lat_batch, len_latent, d)
    y_prompt = x_parts[1].reshape(n_ctx_batch, len_ctx, d)
    new_k = jnp.concatenate(new_k_layers, axis=1)
    new_v = jnp.concatenate(new_v_layers, axis=1)
    return (y_prompt, y_sample, new_k, new_v)
```

```python
import functools
import math

import numpy as np
import jax
import jax.numpy as jnp
from jax import lax
from jax.experimental import pallas as pl
from jax.experimental.pallas import tpu as pltpu

F32 = jnp.float32
BF16 = jnp.bfloat16
HIGHEST = lax.Precision.HIGHEST

GRID_W = 64
N_BANDS = 16
N_HEADS = 16
WIN_ROWS = 8
WIN_COLS = 16
N_GROUPS = 4
EXPERTS_PER_GROUP = 4
LN_EPS = 1e-5
LOG2_E = 1.4426950408889634

LANES = 128
SUBLANES = 8
BF16_ROWS = 16
VMEM_LIMIT_BYTES = 56 * 1024 * 1024

FILTER_PAD = 128
CONV_BLOCK_CTX = 256
CONV_BLOCK_LATENT = 512

MOE_TOKEN_TILE = 512
MOE_ROW_TILE = 512
NBR_QUERY_ROWS = 4


def _cparams(sem):
    return pltpu.CompilerParams(dimension_semantics=sem, vmem_limit_bytes=VMEM_LIMIT_BYTES)


def _dot(a, b, precision=None):
    return jnp.dot(a, b, preferred_element_type=F32, precision=precision)


def _dot_nt(a, b, precision=None):
    return lax.dot_general(a, b, (((1,), (1,)), ((), ())), preferred_element_type=F32,
                           precision=precision)


def _dot3(a, b):
    a_hi = a.astype(BF16)
    a_lo = (a - a_hi.astype(F32)).astype(BF16)
    b_hi = b.astype(BF16)
    b_lo = (b - b_hi.astype(F32)).astype(BF16)
    return _dot(a_hi, b_hi) + (_dot(a_hi, b_lo) + _dot(a_lo, b_hi))


def _silu(x):
    return x / (1.0 + jnp.exp(-x))


def _layer_norm(r, g, b):
    mu = jnp.mean(r, axis=-1, keepdims=True)
    d = r - mu
    var = jnp.mean(d * d, axis=-1, keepdims=True)
    return d * lax.rsqrt(var + LN_EPS) * g + b


def _round_up(x, m):
    return (x + m - 1) // m * m


def _adaln_body(c_ref, w_ref, b_ref, o_ref):
    o_ref[0] = _dot(_silu(c_ref[...]), w_ref[0], HIGHEST) + b_ref[0]


def _adaln(cond, w_ada, b_ada):
    depth, d, d6 = w_ada.shape
    n_cond = cond.shape[0]
    tn = 1024
    return pl.pallas_call(
        _adaln_body,
        grid=(depth, d6 // tn),
        in_specs=[
            pl.BlockSpec((n_cond, d), lambda l, j: (0, 0)),
            pl.BlockSpec((1, d, tn), lambda l, j: (l, 0, j)),
            pl.BlockSpec((1, 1, tn), lambda l, j: (l, 0, j)),
        ],
        out_specs=pl.BlockSpec((1, n_cond, tn), lambda l, j: (l, 0, j)),
        out_shape=jax.ShapeDtypeStruct((depth, n_cond, d6), F32),
        compiler_params=_cparams(("arbitrary", "arbitrary")),
        name="adaln",
    )(cond, w_ada, b_ada.reshape(depth, 1, d6))


class _Mod:
    def __init__(self, table, n_cond, rows_per_cond, n_latent_cond):
        self.table = table
        self.n_cond = n_cond
        self.rows_per_cond = rows_per_cond
        self.n_latent_cond = n_latent_cond

    def spec(self, layer, which, tm, row0=0):
        base = (layer * 6 + which) * self.n_cond
        rpc, nl = self.rows_per_cond, self.n_latent_cond
        d = self.table.shape[-1]

        def index_map(i, *_):
            return (base + jnp.minimum((row0 + i * tm) // rpc, nl), 0, 0)

        return pl.BlockSpec((1, 1, d), index_map)


def _hyena_in_body(x_ref, sh_ref, sc_ref, w0_ref, w1_ref, w2_ref, b0_ref, b1_ref, b2_ref,
                   cw0_ref, cw1_ref, cw2_ref, cb0_ref, cb1_ref, cb2_ref,
                   x0_ref, u_ref, h_ref, *, seq_len):
    j = pl.program_id(1)
    tm = x_ref.shape[0]

    @pl.when(j == 0)
    def _():
        h_ref[...] = (x_ref[...] * (1.0 + sc_ref[0]) + sh_ref[0]).astype(BF16)

    pos =lax.broadcasted_iota(jnp.int32, (tm, 1), 0) & (seq_len - 1)
    first = pos == 0
    last = pos == seq_len - 1

    def part(w_ref, b_ref, cw_ref, cb_ref):
        z = _dot(h_ref[...], w_ref[...]) + b_ref[...]
        z_prev = jnp.where(first, 0.0, pltpu.roll(z, 1, 0))
        z_next = jnp.where(last, 0.0, pltpu.roll(z, tm - 1, 0))
        cw = cw_ref[...]
        return cb_ref[...] + z_prev * cw[0:1] + z * cw[1:2] + z_next * cw[2:3]

    x0_ref[...] = part(w0_ref, b0_ref, cw0_ref, cb0_ref).astype(x0_ref.dtype)
    x1 = part(w1_ref, b1_ref, cw1_ref, cb1_ref)
    v = part(w2_ref, b2_ref, cw2_ref, cb2_ref)
    u_ref[...] = (v * x1).astype(u_ref.dtype)


def _hyena_in(x, mod, layer, w_bf, b_in, conv_w, conv_b, *, seq_len, row0, tm):
    t, d = x.shape
    tn = 256
    assert t % tm == 0 and tm % seq_len == 0 and seq_len & (seq_len - 1) == 0
    nj = d // tn
    b2d = b_in.reshape(1, 3 * d)
    cb2d = conv_b.reshape(1, 3 * d)

    def col(p):
        return lambda i, j: (0, p * nj + j)

    out = jax.ShapeDtypeStruct((t, d), BF16)
    return pl.pallas_call(
        functools.partial(_hyena_in_body, seq_len=seq_len),
        grid=(t // tm, nj),
        in_specs=[pl.BlockSpec((tm, d), lambda i, j: (i, 0)),
                  mod.spec(layer, 0, tm, row0), mod.spec(layer, 1, tm, row0)]
        + [pl.BlockSpec((d, tn), col(p)) for p in range(3)]
        + [pl.BlockSpec((1, tn), col(p)) for p in range(3)]
        + [pl.BlockSpec((3, tn), col(p)) for p in range(3)]
        + [pl.BlockSpec((1, tn), col(p)) for p in range(3)],
        out_specs=[pl.BlockSpec((tm, tn), lambda i, j: (i, j))] * 2,
        out_shape=[out, out],
        scratch_shapes=[pltpu.VMEM((tm, d), BF16)],
        compiler_params=_cparams(("arbitrary", "arbitrary")),
        name=f"hyena_in_{seq_len}",
    )(x, mod.table, mod.table, w_bf, w_bf, w_bf, b2d, b2d, b2d,
      conv_w, conv_w, conv_w, cb2d, cb2d, cb2d)


def _freq_pad(block):
    return _round_up(block + 1, SUBLANES)


def _dft_matrices(block):
    n = 2 * block
    nf = block + 1
    fp = _freq_pad(block)
    f = np.arange(nf, dtype=np.float64)[:, None]
    m = np.arange(n, dtype=np.float64)[None, :]
    ang = 2.0 * np.pi * f * m / n
    fwd = np.zeros((2 * fp, n), np.float64)
    fwd[:nf] = np.cos(ang)
    fwd[fp:fp + nf] = -np.sin(ang)
    a = np.arange(block, dtype=np.float64)[:, None]
    fr = np.arange(nf, dtype=np.float64)[None, :]
    ang_i = 2.0 * np.pi * a * fr / n
    weight = np.full((1, nf), 2.0)
    weight[0, 0] = 1.0
    weight[0, nf - 1] = 1.0
    inv = np.zeros((block, 2 * fp), np.float64)
    inv[:, :nf] = weight * np.cos(ang_i) / n
    inv[:, fp:fp + nf] = -weight * np.sin(ang_i) / n
    return fwd.astype(np.float32), inv.astype(np.float32)


def _lag_tables(nb, seq_len, block):
    n = 2 * block
    lag = np.zeros((2 * nb - 1, n, 1), np.float32)
    valid = np.zeros((2 * nb - 1, n, 1), np.float32)
    m = np.arange(n)
    for dd in range(2 * nb - 1):
        delta = dd - (nb - 1)
        lg = np.where(m < block, block * delta + m, block * delta + m - n)
        ok = (m != block) & (np.abs(lg) <= seq_len - 1)
        lag[dd, :, 0] = np.where(ok, lg, 0)
        valid[dd, :, 0] = ok
    return lag, valid


def _filter_body(lag_ref, valid_ref, bands_ref, w1t_ref, w1c_ref, w1s_ref, b1_ref, w2_ref, b2_ref,
                 freq_ref, w3_ref, dec_ref, dft_ref, k_ref, *, seq_len):
    d = k_ref.shape[-1]
    lag = lag_ref[0]
    pos = jnp.abs(lag)
    tt = pos / float(max(seq_len - 1, 1))
    ang = (2.0 * math.pi / seq_len) * pos * bands_ref[...]
    freq = freq_ref[...]
    pre = (tt * w1t_ref[...] + _dot(jnp.cos(ang), w1c_ref[...], HIGHEST)
           + _dot(-jnp.sin(ang), w1s_ref[...], HIGHEST) + b1_ref[...])
    h = jnp.sin(freq * pre)
    h = jnp.sin(freq * (_dot(h, w2_ref[...], HIGHEST) + b2_ref[...]))
    hf = _dot3(h, w3_ref[:, :d]) * jnp.exp(-tt * jnp.abs(dec_ref[:, :d]))
    hb = _dot3(h, w3_ref[:, d:]) * jnp.exp(-tt * jnp.abs(dec_ref[:, d:]))
    taps = jnp.where(valid_ref[0] > 0.5, jnp.where(lag >= 0.0, hf, hb), 0.0)
    k_ref[0] = _dot3(dft_ref[...], taps)


def _filter_spectra(nb, seq_len, block, f_w1, f_b1, f_w2, f_b2, f_w3, f_freq, decay, dft_fwd):
    hid = f_w1.shape[1]
    d2 = f_w3.shape[1]
    d = d2 // 2
    n = 2 * block
    fp2 = dft_fwd.shape[0]
    lag, valid = _lag_tables(nb, seq_len, block)
    ph = FILTER_PAD - hid
    bands = np.zeros((1, LANES), np.float32)
    bands[0, :N_BANDS] = np.linspace(1e-4, N_BANDS - 1, N_BANDS, dtype=np.float32)
    w1t = jnp.pad(f_w1[0:1], ((0, 0), (0, ph)))
    w1c = jnp.pad(f_w1[1:1 + N_BANDS], ((0, LANES - N_BANDS), (0, ph)))
    w1s = jnp.pad(f_w1[1 + N_BANDS:1 + 2 * N_BANDS], ((0, LANES - N_BANDS), (0, ph)))
    b1 = jnp.pad(f_b1.reshape(1, hid), ((0, 0), (0, ph)))
    w2 = jnp.pad(f_w2, ((0, ph), (0, ph)))
    b2 = jnp.pad(f_b2.reshape(1, hid), ((0, 0), (0, ph)))
    freq = jnp.pad(f_freq.reshape(1, hid), ((0, 0), (0, ph)))
    w3 = jnp.pad(f_w3, ((0, ph), (0, 0)))
    dec = decay.reshape(1, d2)
    nd = 2 * nb - 1

    def whole(shape):
        return pl.BlockSpec(shape, lambda dd: (0,) * len(shape))

    return pl.pallas_call(
        functools.partial(_filter_body, seq_len=seq_len),
        grid=(nd,),
        in_specs=[
            pl.BlockSpec((1, n, 1), lambda dd: (dd, 0, 0)),
            pl.BlockSpec((1, n, 1), lambda dd: (dd, 0, 0)),
            whole((1, LANES)), whole((1, FILTER_PAD)), whole((LANES, FILTER_PAD)),
            whole((LANES, FILTER_PAD)), whole((1, FILTER_PAD)), whole((FILTER_PAD, FILTER_PAD)),
            whole((1, FILTER_PAD)), whole((1, FILTER_PAD)),
            whole((FILTER_PAD, d2)), whole((1, d2)), whole((fp2, n)),
        ],
        out_specs=pl.BlockSpec((1, fp2, d), lambda dd: (dd, 0, 0)),
        out_shape=jax.ShapeDtypeStruct((nd, fp2, d), F32),
        compiler_params=_cparams(("arbitrary",)),
        name=f"hyena_filter_{seq_len}",
    )(jnp.asarray(lag), jnp.asarray(valid), jnp.asarray(bands), w1t, w1c, w1s, b1, w2, b2, freq,
      w3, dec, dft_fwd)


def _hyena_conv_body(u_ref, x0_ref, k_ref, dsk_ref, dftu_ref, idft_ref, o_ref, uf_ref, yf_ref,
                     *, n_seq, nb, block):
    fp = k_ref.shape[1] // 2
    ft_rows = SUBLANES

    def fwd(blk, carry):
        rows = pl.ds(pl.multiple_of(blk * block, block), block)
        uf_ref[blk] = _dot(dftu_ref[...], u_ref[rows, :])
        return carry

    lax.fori_loop(0, n_seq * nb, fwd, 0)

    def freq_tile(ft, carry):
        r = pl.multiple_of(ft * ft_rows, ft_rows)
        re = pl.ds(r, ft_rows)
        im = pl.ds(fp + r, ft_rows)
        k_re = [k_ref[dd, re, :] for dd in range(2 * nb - 1)]
        k_im = [k_ref[dd, im, :] for dd in range(2 * nb - 1)]
        for s in range(n_seq):
            u_re = [uf_ref[s * nb + j, re, :] for j in range(nb)]
            u_im = [uf_ref[s * nb + j, im, :] for j in range(nb)]
            for i in range(nb):
                acc_re = None
                acc_im = None
                for j in range(nb):
                    dd = i - j + (nb - 1)
                    t_re = k_re[dd] * u_re[j] - k_im[dd] * u_im[j]
                    t_im = k_re[dd] * u_im[j] + k_im[dd] * u_re[j]
                    acc_re = t_re if acc_re is None else acc_re + t_re
                    acc_im = t_im if acc_im is None else acc_im + t_im
                yf_ref[s * nb + i, re, :] = acc_re
                yf_ref[s * nb + i, im, :] = acc_im
        return carry

    lax.fori_loop(0, fp // ft_rows, freq_tile, 0)

    def inv(blk, carry):
        rows = pl.ds(pl.multiple_of(blk * block, block), block)
        y = _dot(idft_ref[...], yf_ref[blk].astype(BF16))
        u_blk = u_ref[rows, :].astype(F32)
        o_ref[rows, :] = ((y + u_blk * dsk_ref[...]) * x0_ref[rows, :].astype(F32)).astype(o_ref.dtype)
        return carry

    lax.fori_loop(0, n_seq * nb, inv, 0)


def _hyena_conv(u, x0, spectra, d_skip, dft_fwd, dft_inv, *, block, n_seq, nb):
    t, d = u.shape
    ct = 256
    tm = n_seq * nb * block
    nd = 2 * nb - 1
    fp2 = dft_fwd.shape[0]
    rows = lambda c, b: (b, c)
    return pl.pallas_call(
        functools.partial(_hyena_conv_body, n_seq=n_seq, nb=nb, block=block),
        grid=(d // ct, t // tm),
        in_specs=[
            pl.BlockSpec((tm, ct), rows),
            pl.BlockSpec((tm, ct), rows),
            pl.BlockSpec((nd, fp2, ct), lambda c, b: (0, 0, c)),
            pl.BlockSpec((1, ct), lambda c, b: (0, c)),
            pl.BlockSpec((fp2, block), lambda c, b: (0, 0)),
            pl.BlockSpec((block, fp2), lambda c, b: (0, 0)),
        ],
        out_specs=pl.BlockSpec((tm, ct), rows),
        out_shape=jax.ShapeDtypeStruct((t, d), BF16),
        scratch_shapes=[pltpu.VMEM((n_seq * nb, fp2, ct), F32),
                        pltpu.VMEM((n_seq * nb, fp2, ct), F32)],
        compiler_params=_cparams(("arbitrary", "arbitrary")),
        name=f"hyena_conv_nb{nb}",
    )(u, x0, spectra, d_skip.reshape(1, d), dft_fwd[:, :block].astype(BF16), dft_inv.astype(BF16))


def _out_proj_body(*refs, alpha, split_tile, x_parts):
    a_refs = refs[0:2]
    x_refs = refs[2:2 + x_parts]
    (w_ref, b_ref, gate_ref, g_ref, beta_ref, sh_ref, sc_ref, rw_ref, rb_ref,
     o_ref, t_ref, route_ref, cnt_ref) = refs[2 + x_parts:]
    i = pl.program_id(0)

    def run(a_ref, x_ref):
        o = _dot(a_ref[...], w_ref[...]) + b_ref[...]
        r = alpha * x_ref[...] + gate_ref[0] * o
        o_ref[...] = _layer_norm(r, g_ref[...], beta_ref[...])

    @pl.when(i < split_tile)
    def _():
        run(a_refs[0], x_refs[0])

    @pl.when(i >= split_tile)
    def _():
        run(a_refs[1], x_refs[-1])

    t = o_ref[...] * (1.0 + sc_ref[0]) + sh_ref[0]
    t_ref[...] = t.astype(BF16)
    route, counts = _route_tokens(t, rw_ref[...], rb_ref[...])
    route_ref[...] = route
    cnt_ref[0] = jnp.broadcast_to(counts, cnt_ref.shape[1:])


def _out_proj_route(a_parts, x_parts, w, b, mod, layer, ln_g, ln_b, alpha, w_group, b_group, w_expert,
                    b_expert):
    d = a_parts[0].shape[1]
    tm = MOE_TOKEN_TILE
    split_tile = a_parts[0].shape[0] // tm
    t = a_parts[0].shape[0] + a_parts[1].shape[0]
    pad = LANES - N_GROUPS - w_expert.shape[1]
    rw = jnp.pad(jnp.concatenate([w_group, w_expert], axis=1), ((0, 0), (0, pad)))
    rb = jnp.pad(jnp.concatenate([b_group, b_expert]), (0, pad)).reshape(1, LANES)
    first = lambda i: (jnp.minimum(i, split_tile - 1), 0)
    second = lambda i: (jnp.maximum(i - split_tile, 0), 0)
    row = lambda i: (i, 0)
    const = lambda i: (0, 0)
    part_specs = [pl.BlockSpec((tm, d), first), pl.BlockSpec((tm, d), second)]
    x_specs = part_specs if len(x_parts) == 2 else [pl.BlockSpec((tm, d), row)]
    return pl.pallas_call(
        functools.partial(_out_proj_body, alpha=alpha, split_tile=split_tile, x_parts=len(x_parts)),
        grid=(t // tm,),
        in_specs=part_specs + x_specs
        + [pl.BlockSpec((d, d), const), pl.BlockSpec((1, d), const), mod.spec(layer, 2, tm),
           pl.BlockSpec((1, d), const), pl.BlockSpec((1, d), const),
           mod.spec(layer, 3, tm), mod.spec(layer, 4, tm),
           pl.BlockSpec((d, LANES), const), pl.BlockSpec((1, LANES), const)],
        out_specs=[pl.BlockSpec((tm, d), row), pl.BlockSpec((tm, d), row), pl.BlockSpec((tm, LANES), row),
                   pl.BlockSpec((1, SUBLANES, LANES), lambda i: (i, 0, 0))],
        out_shape=[jax.ShapeDtypeStruct((t, d), F32), jax.ShapeDtypeStruct((t, d), BF16),
                   jax.ShapeDtypeStruct((t, LANES), F32),
                   jax.ShapeDtypeStruct((t // tm, SUBLANES, LANES), F32)],
        compiler_params=_cparams(("arbitrary",)),
        name="out_proj_ln_route",
    )(*a_parts, *x_parts, w.astype(BF16), b.reshape(1, d), mod.table,
      ln_g.reshape(1, d), ln_b.reshape(1, d), mod.table, mod.table, rw, rb)


def _route_tokens(t, w, b):
    tm = t.shape[0]
    lg = _dot3(t, w) + b
    lane = lax.broadcasted_iota(jnp.int32, lg.shape, 1)
    neg = -jnp.inf
    far = jnp.int32(LANES)
    g_mask = lane < N_GROUPS
    gl = jnp.where(g_mask, lg, neg)
    g_max = jnp.max(gl, axis=-1, keepdims=True)
    g_sel = jnp.min(jnp.where(gl == g_max, lane, far), axis=-1, keepdims=True)
    g_w = 1.0 / jnp.sum(jnp.where(g_mask, jnp.exp(lg - g_max), 0.0), axis=-1, keepdims=True)
    e_lo = N_GROUPS + EXPERTS_PER_GROUP * g_sel
    e_mask = (lane >= e_lo) & (lane < e_lo + EXPERTS_PER_GROUP)
    el = jnp.where(e_mask, lg, neg)
    m1 = jnp.max(el, axis=-1, keepdims=True)
    i1 = jnp.min(jnp.where(el == m1, lane, far), axis=-1, keepdims=True)
    el2 = jnp.where(lane == i1, neg, el)
    m2 = jnp.max(el2, axis=-1, keepdims=True)
    i2 = jnp.min(jnp.where(el2 == m2, lane, far), axis=-1, keepdims=True)
    ratio = jnp.exp(m2 - m1)
    w1 = 1.0 / (1.0 + ratio)
    w2 = ratio / (1.0 + ratio)
    e1 = i1 - N_GROUPS
    e2 = i2 - N_GROUPS
    hit = (lane == e1) | (lane == e2)
    counts = jnp.sum(hit.astype(F32), axis=0, keepdims=True)
    run_len = jnp.floor((counts + (BF16_ROWS - 1)) * (1.0 / BF16_ROWS)) * BF16_ROWS
    lower = (lax.broadcasted_iota(jnp.int32, (LANES, LANES), 0)
             < lax.broadcasted_iota(jnp.int32, (LANES, LANES), 1)).astype(F32)
    starts = _dot(jnp.broadcast_to(run_len, (SUBLANES, LANES)), lower, HIGHEST)[0:1]
    earlier = (lax.broadcasted_iota(jnp.int32, (tm, tm), 1)
               < lax.broadcasted_iota(jnp.int32, (tm, tm), 0)).astype(BF16)
    pos = _dot(earlier, hit.astype(BF16)) + starts
    p1 = jnp.sum(jnp.where(lane == e1, pos, 0.0), axis=-1, keepdims=True)
    p2 = jnp.sum(jnp.where(lane == e2, pos, 0.0), axis=-1, keepdims=True)
    fields = (e1.astype(F32), e2.astype(F32), g_w * w1, g_w * w2, p1, p2)
    route = jnp.zeros(lg.shape, F32)
    for k, val in enumerate(fields):
        route = jnp.where(lane == k, val, route)
    return route, counts


def _moe_layout(counts, n_experts, n_row_tiles_max, chunk_slots):
    cnt = counts[:, 0, :n_experts].astype(jnp.int32)
    run_len = (cnt + (BF16_ROWS - 1)) // BF16_ROWS * BF16_ROWS
    total = jnp.sum(run_len, axis=0)
    region = (total + (MOE_ROW_TILE - 1)) // MOE_ROW_TILE * MOE_ROW_TILE
    region_end = jnp.cumsum(region)
    expert_start = region_end - region
    run_start = expert_start[None, :] + jnp.cumsum(run_len, axis=0) - run_len
    n_used = (region_end[-1] // MOE_ROW_TILE).reshape(1).astype(jnp.int32)
    tile_idx = jnp.arange(n_row_tiles_max, dtype=jnp.int32)
    tile_expert = jnp.minimum(
        jnp.sum((tile_idx[:, None] >= (region_end // MOE_ROW_TILE)[None, :]).astype(jnp.int32), axis=1),
        n_experts - 1).astype(jnp.int32)
    used_rows = region_end[-1:]
    tail_chunks = (n_row_tiles_max * MOE_ROW_TILE - used_rows) // (MOE_ROW_TILE // 2)
    local_end = jnp.cumsum(run_len, axis=1)
    pos = jnp.arange(chunk_slots, dtype=jnp.int32) * BF16_ROWS
    run_of_chunk = jnp.sum((pos[None, :, None] >= local_end[:, None, :]).astype(jnp.int32), axis=-1)
    in_run = run_of_chunk[:, :, None] == jnp.arange(n_experts, dtype=jnp.int32)[None, None, :]
    shift = (run_start - (local_end - run_len))[:, None, :]
    chunk_row = jnp.sum(jnp.where(in_run, shift, 0), axis=-1) + pos[None, :]
    chunk_row = jnp.where(pos[None, :] < local_end[:, -1:], chunk_row, 0)
    return dict(chunk_row=chunk_row.reshape(-1).astype(jnp.int32),
                n_chunks=(local_end[:, -1] // BF16_ROWS).astype(jnp.int32),
                gap_start=jnp.concatenate([expert_start + total, used_rows]).astype(jnp.int32),
                gap_len=jnp.concatenate([region - total, tail_chunks]).astype(jnp.int32),
                tile_expert=tile_expert, n_used=n_used)


def _run_chunk_loops(chunk_copy, n_chunks):
    def start(c, carry):
        chunk_copy(c).start()
        return carry

    def wait(c, carry):
        chunk_copy(c).wait()
        return carry

    return (lambda: lax.fori_loop(0, n_chunks, start, 0)), (lambda: lax.fori_loop(0, n_chunks, wait, 0))


def _chunk_copies(src_ref, src_off, dst_ref, dst_off, n, sem, max_chunk, advance_src=True):
    out = []
    off = jnp.int32(0)
    bit = max_chunk
    while bit >= BF16_ROWS:
        take = n & bit
        s = pl.multiple_of(src_off + off, BF16_ROWS) if advance_src else src_off
        t = pl.multiple_of(dst_off + off, BF16_ROWS)
        cp = pltpu.make_async_copy(src_ref.at[pl.ds(s, bit)], dst_ref.at[pl.ds(t, bit)], sem)
        out.append((take != 0, cp))
        off = off + take
        bit //= 2
    return out


def _start_all(copies):
    for cond, cp in copies:
        pl.when(cond)(cp.start)


def _wait_all(copies):
    for cond, cp in copies:
        pl.when(cond)(cp.wait)


def _dispatch_body(cd_ref, nch_ref, gs_ref, gl_ref, t_ref, route_ref, xg_ref, sorted_ref, zero_ref, sem,
                   *, n_experts):
    tile = pl.program_id(0)
    tm = t_ref.shape[0]
    n_rows = sorted_ref.shape[0]
    slots = n_rows // BF16_ROWS
    route = route_ref[...]
    lane = lax.broadcasted_iota(jnp.int32, (tm, LANES), 1)
    both = jnp.where(lane == 0, route[:, 4:5], jnp.where(lane == 1, route[:, 5:6], 0.0))
    pick = (lax.broadcasted_iota(jnp.int32, (SUBLANES, LANES), 0)
            == lax.broadcasted_iota(jnp.int32, (SUBLANES, LANES), 1)).astype(F32)
    as_rows = _dot_nt(pick, both, HIGHEST)
    r_iota = lax.broadcasted_iota(jnp.int32, (n_rows, tm), 0).astype(F32)
    select = ((r_iota == as_rows[0:1]) | (r_iota == as_rows[1:2])).astype(BF16)
    sorted_ref[...] = _dot(select, t_ref[...]).astype(BF16)

    def chunk_copy(c):
        src = pl.multiple_of(c * BF16_ROWS, BF16_ROWS)
        dst = pl.multiple_of(cd_ref[tile * slots + c], BF16_ROWS)
        return pltpu.make_async_copy(sorted_ref.at[pl.ds(src, BF16_ROWS)],
                                     xg_ref.at[pl.ds(dst, BF16_ROWS)], sem)

    start_chunks, wait_chunks = _run_chunk_loops(chunk_copy, nch_ref[tile])
    start_chunks()
    wait_chunks()

    @pl.when(tile == pl.num_programs(0) - 1)
    def _():
        zero_ref[...] = jnp.zeros_like(zero_ref)
        fills = []
        for e in range(n_experts):
            fills += _chunk_copies(zero_ref, 0, xg_ref, gs_ref[e], gl_ref[e], sem, zero_ref.shape[0],
                                   advance_src=False)
        _start_all(fills)
        _wait_all(fills)
        z_rows = zero_ref.shape[0]
        tail0 = gs_ref[n_experts]

        def tail_copy(c):
            dst = pl.multiple_of(tail0 + c * z_rows, z_rows)
            return pltpu.make_async_copy(zero_ref, xg_ref.at[pl.ds(dst, z_rows)], sem)

        def start(c, carry):
            tail_copy(c).start()
            return carry

        def wait(c, carry):
            tail_copy(c).wait()
            return carry

        lax.fori_loop(0, gl_ref[n_experts], start, 0)
        lax.fori_loop(0, gl_ref[n_experts], wait, 0)


def _sorted_rows(tm, n_experts):
    return _round_up(2 * tm + n_experts * (BF16_ROWS - 1), LANES)


def _dispatch(tb, route, layout, n_experts, n_rows_max):
    t, d = tb.shape
    tm = MOE_TOKEN_TILE
    row = lambda i, *_: (i, 0)
    grid_spec = pltpu.PrefetchScalarGridSpec(
        num_scalar_prefetch=4,
        grid=(t // tm,),
        in_specs=[pl.BlockSpec((tm, d), row), pl.BlockSpec((tm, LANES), row)],
        out_specs=pl.BlockSpec(memory_space=pl.ANY),
        scratch_shapes=[pltpu.VMEM((_sorted_rows(tm, n_experts), d), BF16),
                        pltpu.VMEM((MOE_ROW_TILE // 2, d), BF16),
                        pltpu.SemaphoreType.DMA(())],
    )
    return pl.pallas_call(
        functools.partial(_dispatch_body, n_experts=n_experts),
        grid_spec=grid_spec,
        out_shape=jax.ShapeDtypeStruct((n_rows_max, d), BF16),
        compiler_params=_cparams(("arbitrary",)),
        name="moe_dispatch",
    )(layout["chunk_row"], layout["n_chunks"], layout["gap_start"], layout["gap_len"], tb, route)


def _experts_body(te_ref, nu_ref, x_ref, wu_ref, wd_ref, y_ref, wub_ref, wdb_ref, *, d_expert):
    g = pl.program_id(0)

    @pl.when(g < nu_ref[0])
    def _():
        prev = te_ref[jnp.maximum(g - 1, 0)]

        @pl.when((g == 0) | (te_ref[g] != prev))
        def _():
            wub_ref[...] = wu_ref[0, 0].astype(BF16)
            wdb_ref[...] = wd_ref[0, 0].astype(BF16)

        ab = _dot(x_ref[...], wub_ref[...])
        h = _silu(ab[:, :d_expert]) * ab[:, d_expert:]
        y_ref[...] = _dot(h.astype(BF16), wdb_ref[...]).astype(y_ref.dtype)

    @pl.when(g >= nu_ref[0])
    def _():
        y_ref[...] = jnp.zeros_like(y_ref)


def _experts(xg, w_up, w_down, layer, layout):
    n_rows, d = xg.shape
    d_up = w_up.shape[-1]
    d_expert = d_up // 2
    tm = MOE_ROW_TILE
    used = lambda g, nu: jnp.maximum(jnp.minimum(g, nu[0] - 1), 0)
    grid_spec = pltpu.PrefetchScalarGridSpec(
        num_scalar_prefetch=2,
        grid=(n_rows // tm,),
        in_specs=[pl.BlockSpec((tm, d), lambda g, te, nu: (used(g, nu), 0)),
                  pl.BlockSpec((1, 1, d, d_up), lambda g, te, nu: (layer, te[used(g, nu)], 0, 0)),
                  pl.BlockSpec((1, 1, d_expert, d), lambda g, te, nu: (layer, te[used(g, nu)], 0, 0))],
        out_specs=pl.BlockSpec((tm, d), lambda g, te, nu: (g, 0)),
        scratch_shapes=[pltpu.VMEM((d, d_up), BF16), pltpu.VMEM((d_expert, d), BF16)],
    )
    return pl.pallas_call(
        functools.partial(_experts_body, d_expert=d_expert),
        grid_spec=grid_spec,
        out_shape=jax.ShapeDtypeStruct((n_rows, d), BF16),
        compiler_params=_cparams(("arbitrary",)),
        name="moe_experts",
    )(layout["tile_expert"], layout["n_used"], xg, w_up, w_down)


def _combine_body(cd_ref, nch_ref, yg_ref, route_ref, x_ref, gate_ref, g_ref, beta_ref, *rest,
                  alpha, split_tile):
    if split_tile is None:
        o_ref, ybuf_ref, sem = rest
    else:
        o_ref, o2_ref, ybuf_ref, sem = rest
    tile = pl.program_id(0)
    tm = x_ref.shape[0]
    n_rows = ybuf_ref.shape[0]
    slots = n_rows // BF16_ROWS
    ybuf_ref[2 * tm:, :] = jnp.zeros((n_rows - 2 * tm, ybuf_ref.shape[1]), ybuf_ref.dtype)

    def chunk_copy(c):
        src = pl.multiple_of(cd_ref[tile * slots + c], BF16_ROWS)
        dst = pl.multiple_of(c * BF16_ROWS, BF16_ROWS)
        return pltpu.make_async_copy(yg_ref.at[pl.ds(src, BF16_ROWS)],
                                     ybuf_ref.at[pl.ds(dst, BF16_ROWS)], sem)

    start_chunks, wait_chunks = _run_chunk_loops(chunk_copy, nch_ref[tile])
    start_chunks()

    route = route_ref[...]
    r_lane = lax.broadcasted_iota(jnp.int32, (tm, n_rows), 1).astype(F32)
    cmat = (jnp.where(r_lane == route[:, 4:5], route[:, 2:3], 0.0)
            + jnp.where(r_lane == route[:, 5:6], route[:, 3:4], 0.0)).astype(BF16)
    wait_chunks()
    y = _dot(cmat, ybuf_ref[...])
    r = alpha * x_ref[...] + gate_ref[0] * y
    res = _layer_norm(r, g_ref[...], beta_ref[...])
    if split_tile is None:
        o_ref[...] = res
    else:
        @pl.when(tile < split_tile)
        def _():
            o_ref[...] = res

        @pl.when(tile >= split_tile)
        def _():
            o2_ref[...] = res


def _combine(yg, route, x, layout, mod, layer, ln_g, ln_b, alpha, n_experts, split_rows=None):
    t, d = x.shape
    tm = MOE_TOKEN_TILE
    row = lambda i, *_: (i, 0)
    const = lambda i, *_: (0, 0)
    if split_rows is None:
        split_tile = None
        out_specs = pl.BlockSpec((tm, d), row)
        out_shape = jax.ShapeDtypeStruct((t, d), F32)
    else:
        assert split_rows % tm == 0
        split_tile = split_rows // tm
        out_specs = [pl.BlockSpec((tm, d), lambda i, *_: (jnp.minimum(i, split_tile - 1), 0)),
                     pl.BlockSpec((tm, d), lambda i, *_: (jnp.maximum(i - split_tile, 0), 0))]
        out_shape = [jax.ShapeDtypeStruct((split_rows, d), F32),
                     jax.ShapeDtypeStruct((t - split_rows, d), F32)]
    grid_spec = pltpu.PrefetchScalarGridSpec(
        num_scalar_prefetch=2,
        grid=(t // tm,),
        in_specs=[pl.BlockSpec(memory_space=pl.ANY), pl.BlockSpec((tm, LANES), row),
                  pl.BlockSpec((tm, d), row), mod.spec(layer, 5, tm),
                  pl.BlockSpec((1, d), const), pl.BlockSpec((1, d), const)],
        out_specs=out_specs,
        scratch_shapes=[pltpu.VMEM((_sorted_rows(tm, n_experts), d), BF16),
                        pltpu.SemaphoreType.DMA(())],
    )
    return pl.pallas_call(
        functools.partial(_combine_body, alpha=alpha, split_tile=split_tile),
        grid_spec=grid_spec,
        out_shape=out_shape,
        compiler_params=_cparams(("arbitrary",)),
        name="moe_combine_ln",
    )(layout["chunk_row"], layout["n_chunks"], yg, route, x, mod.table,
      ln_g.reshape(1, d), ln_b.reshape(1, d))


def _moe_layer(x, tb, route, counts, mod, layer, w_up, w_down, ln_g, ln_b, alpha, split_rows=None):
    t, d = x.shape
    n_e = w_up.shape[1]
    n_tok_tiles = t // MOE_TOKEN_TILE
    max_rows = 2 * t + n_tok_tiles * n_e * (BF16_ROWS - 1) + n_e * (MOE_ROW_TILE - BF16_ROWS)
    n_rows_max = _round_up(max_rows, MOE_ROW_TILE)
    layout = _moe_layout(counts, n_e, n_rows_max // MOE_ROW_TILE,
                         _sorted_rows(MOE_TOKEN_TILE, n_e) // BF16_ROWS)
    xg = _dispatch(tb, route, layout, n_e, n_rows_max)
    yg = _experts(xg, w_up, w_down, layer, layout)
    return _combine(yg, route, x, layout, mod, layer, ln_g, ln_b, alpha, n_e, split_rows)


def _qkv_body(y_ref, sh_ref, sc_ref, w_ref, b_ref, qkv_ref, nk_ref, nv_ref, *, n_latent_tiles, d):
    i = pl.program_id(0)
    h = (y_ref[...] * (1.0 + sc_ref[0]) + sh_ref[0]).astype(BF16)
    z = _dot(h, w_ref[...]) + b_ref[...]
    qkv_ref[...] = z.astype(BF16)

    @pl.when(i >= n_latent_tiles)
    def _():
        hd = nk_ref.shape[-1]
        for hh in range(nk_ref.shape[2]):
            nk_ref[0, 0, hh] = z[:, d + hh * hd:d + (hh + 1) * hd]
            nv_ref[0, 0, hh] = z[:, 2 * d + hh * hd:2 * d + (hh + 1) * hd]


def _qkv(y, mod, layer, w, b, *, n_latent_rows, n_ctx_batch, len_ctx):
    t, d = y.shape
    tm = len_ctx
    hd = d // N_HEADS
    n_lat = n_latent_rows // tm
    kv_map = lambda i: (jnp.maximum(i - n_lat, 0), 0, 0, 0, 0)
    kv_shape = jax.ShapeDtypeStruct((n_ctx_batch, 1, N_HEADS, len_ctx, hd), F32)
    return pl.pallas_call(
        functools.partial(_qkv_body, n_latent_tiles=n_lat, d=d),
        grid=(t // tm,),
        in_specs=[pl.BlockSpec((tm, d), lambda i: (i, 0)), mod.spec(layer, 0, tm), mod.spec(layer, 1, tm),
                  pl.BlockSpec((d, 3 * d), lambda i: (0, 0)), pl.BlockSpec((1, 3 * d), lambda i: (0, 0))],
        out_specs=[pl.BlockSpec((tm, 3 * d), lambda i: (i, 0)),
                   pl.BlockSpec((1, 1, N_HEADS, len_ctx, hd), kv_map),
                   pl.BlockSpec((1, 1, N_HEADS, len_ctx, hd), kv_map)],
        out_shape=[jax.ShapeDtypeStruct((t, 3 * d), BF16), kv_shape, kv_shape],
        compiler_params=_cparams(("arbitrary",)),
        name="attn_qkv",
    )(y, mod.table, mod.table, w.astype(BF16), b.reshape(1, 3 * d))


def _ctx_attn_body(qkv_ref, o_ref, *, d, scale):
    pair = 2 * (d // N_HEADS)
    left = lax.broadcasted_iota(jnp.int32, (1, pair), 1) < pair // 2
    c = scale * LOG2_E
    for hp in range(d // pair):
        cols = slice(hp * pair, (hp + 1) * pair)
        q2 = qkv_ref[:, cols]
        k2 = qkv_ref[:, d + hp * pair:d + (hp + 1) * pair]
        v2 = qkv_ref[:, 2 * d + hp * pair:2 * d + (hp + 1) * pair]
        outs = []
        for mask in (left, ~left):
            s = _dot_nt(jnp.where(mask, q2, jnp.zeros_like(q2)), k2) * c
            e = jnp.exp2(s - jnp.max(s, axis=-1, keepdims=True))
            p = (e / jnp.sum(e, axis=-1, keepdims=True)).astype(BF16)
            outs.append(_dot(p, v2))
        o_ref[:, cols] = jnp.where(left, outs[0], outs[1]).astype(o_ref.dtype)


def _ctx_attn(qkv, *, row_block0, n_batch, len_ctx):
    d3 = qkv.shape[1]
    d = d3 // 3
    scale = (d // N_HEADS) ** -0.5
    return pl.pallas_call(
        functools.partial(_ctx_attn_body, d=d, scale=scale),
        grid=(n_batch,),
        in_specs=[pl.BlockSpec((len_ctx, d3), lambda b: (row_block0 + b, 0))],
        out_specs=pl.BlockSpec((len_ctx, d), lambda b: (b, 0)),
        out_shape=jax.ShapeDtypeStruct((n_batch * len_ctx, d), BF16),
        compiler_params=_cparams(("arbitrary",)),
        name="ctx_attention",
    )(qkv)


def _nbr_bias_body(rpb_ref, onehot_ref, band_ref, o_ref):
    picked = _dot(rpb_ref[...], onehot_ref[...], HIGHEST)
    o_ref[...] = jnp.where(band_ref[...] > 0.5, picked * LOG2_E, -jnp.inf)


def _nbr_bias_tiles(rpb):
    n_h, n_dr, n_dc = rpb.shape
    col = np.arange(GRID_W)
    col_start = np.clip(col - WIN_COLS // 2, 0, GRID_W - WIN_COLS)
    in_band = (col[None, :] >= col_start[:, None]) & (col[None, :] < col_start[:, None] + WIN_COLS)
    dc_idx = np.clip(col[None, :] - col[:, None], -(WIN_COLS - 1), WIN_COLS - 1) + (WIN_COLS - 1)
    n_pairs = GRID_W * GRID_W
    onehot = np.zeros((LANES, n_pairs), np.float32)
    onehot[dc_idx.reshape(-1), np.arange(n_pairs)] = 1.0
    band = in_band.reshape(1, n_pairs).astype(np.float32)
    rows = n_h * n_dr
    rpb2d = jnp.pad(rpb.reshape(rows, n_dc), ((0, 0), (0, LANES - n_dc)))
    whole = lambda shape: pl.BlockSpec(shape, lambda i: (0,) * len(shape))
    cols = pl.pallas_call(
        _nbr_bias_body,
        grid=(1,),
        in_specs=[whole((rows, LANES)), whole((LANES, n_pairs)), whole((1, n_pairs))],
        out_specs=whole((rows, n_pairs)),
        out_shape=jax.ShapeDtypeStruct((rows, n_pairs), F32),
        compiler_params=_cparams(("arbitrary",)),
        name="nbr_bias",
    )(rpb2d, jnp.asarray(onehot), jnp.asarray(band))
    cols = cols.reshape(n_h, n_dr, GRID_W, GRID_W)
    n_entries = 2 * WIN_ROWS
    masked = jnp.full((n_h, n_entries - n_dr, GRID_W, GRID_W), -jnp.inf, F32)
    cols = jnp.concatenate([cols, masked], axis=1)
    zeros = jnp.zeros_like(cols)
    left = jnp.concatenate([cols, zeros], axis=-1)
    right = jnp.concatenate([zeros, cols], axis=-1)
    return jnp.concatenate([left, right], axis=1)


def _nbr_attn_body(q_ref, k_ref, v_ref, ck_ref, cv_ref, tab_ref, o_ref, *, rows, kh, scale):
    hd = ck_ref.shape[-1]
    qr = NBR_QUERY_ROWS
    span = qr + kh
    n_q = qr * GRID_W
    n_k = span * GRID_W
    masked = 2 * WIN_ROWS - 1
    c = scale * LOG2_E
    left = lax.broadcasted_iota(jnp.int32, (1, 2 * hd), 1) < hd
    kc2 = jnp.concatenate([ck_ref[0, 0, 0], ck_ref[0, 0, 1]], axis=1).astype(BF16)
    vc2 = jnp.concatenate([cv_ref[0, 0, 0], cv_ref[0, 0, 1]], axis=1).astype(BF16)

    def block(qb):
        r0 = qb * qr
        s0 = jnp.clip(r0 - kh // 2, 0, rows - span)
        q_rows = pl.ds(pl.multiple_of(r0 * GRID_W, n_q), n_q)
        k_rows = pl.ds(pl.multiple_of(s0 * GRID_W, GRID_W), n_k)
        q2 = q_ref[q_rows, :]
        kw2 = k_ref[k_rows, :]
        vw2 = v_ref[k_rows, :]
        outs = []
        for hh, mask in enumerate((left, ~left)):
            bias_rows = []
            for rq in range(qr):
                r = r0 + rq
                ws = jnp.clip(r - kh // 2, 0, rows - kh)
                tiles = []
                for kp in range(span // 2):
                    ke = s0 + 2 * kp
                    ko = ke + 1
                    ie = jnp.where((ke >= ws) & (ke < ws + kh), ke - r + (WIN_ROWS - 1), masked)
                    io = jnp.where((ko >= ws) & (ko < ws + kh), ko - r + (WIN_ROWS - 1), masked)
                    tiles.append(tab_ref[hh, ie] + tab_ref[hh, 2 * WIN_ROWS + io])
                bias_rows.append(jnp.concatenate(tiles, axis=1))
            bias = jnp.concatenate(bias_rows, axis=0)
            qh = jnp.where(mask, q2, jnp.zeros_like(q2))
            s_loc = _dot_nt(qh, kw2) * c + bias
            s_ctx = _dot_nt(qh, kc2) * c
            m = jnp.maximum(jnp.max(s_loc, axis=-1, keepdims=True), jnp.max(s_ctx, axis=-1, keepdims=True))
            e_loc = jnp.exp2(s_loc - m)
            e_ctx = jnp.exp2(s_ctx - m)
            denom = jnp.sum(e_loc, axis=-1, keepdims=True) + jnp.sum(e_ctx, axis=-1, keepdims=True)
            outs.append((_dot(e_loc.astype(BF16), vw2) + _dot(e_ctx.astype(BF16), vc2)) / denom)
        o_ref[q_rows, :] = jnp.where(left, outs[0], outs[1]).astype(o_ref.dtype)

    def two_blocks(i, carry):
        block(2 * i)
        block(2 * i + 1)
        return carry

    lax.fori_loop(0, rows // qr // 2, two_blocks, 0)


def _nbr_attn(qkv, cache_k, cache_v, tab, layer_j, *, n_batch, len_latent):
    d3 = qkv.shape[1]
    t = n_batch * len_latent
    d = d3 // 3
    hd = d // N_HEADS
    rows = len_latent // GRID_W
    kh = min(WIN_ROWS, rows)
    assert rows % (2 * NBR_QUERY_ROWS) == 0 and rows >= NBR_QUERY_ROWS + kh and (NBR_QUERY_ROWS + kh) % 2 == 0
    assert 2 * hd == LANES
    n_hp = N_HEADS // 2
    pc = cache_k.shape[3]
    qmap = lambda p: (lambda b, h: (b, p * n_hp + h))
    cmap = lambda b, h: (b, layer_j, h, 0, 0)
    return pl.pallas_call(
        functools.partial(_nbr_attn_body, rows=rows, kh=kh, scale=hd ** -0.5),
        grid=(n_batch, n_hp),
        in_specs=[pl.BlockSpec((len_latent, 2 * hd), qmap(0)),
                  pl.BlockSpec((len_latent, 2 * hd), qmap(1)),
                  pl.BlockSpec((len_latent, 2 * hd), qmap(2)),
                  pl.BlockSpec((1, 1, 2, pc, hd), cmap),
                  pl.BlockSpec((1, 1, 2, pc, hd), cmap),
                  pl.BlockSpec((2,) + tab.shape[1:], lambda b, h: (h, 0, 0, 0))],
        out_specs=pl.BlockSpec((len_latent, 2 * hd), lambda b, h: (b, h)),
        out_shape=jax.ShapeDtypeStruct((t, d), BF16),
        compiler_params=_cparams(("arbitrary", "arbitrary")),
        name="nbr_attention",
    )(qkv, qkv, qkv, cache_k, cache_v, tab)


def kernel(x_prompt, x_sample, cache_k, cache_v, c, c_ctx, w_ada, b_ada, ln_g, ln_b, hy_w_in, hy_b_in, hy_conv_w, hy_conv_b, hy_f_w1, hy_f_b1, hy_f_w2, hy_f_b2, hy_f_w3, hy_f_freq, hy_decay, hy_d, hy_w_out, hy_b_out, na_w_qkv, na_b_qkv, na_rpb, na_w_out, na_b_out, moe_w_group, moe_b_group, moe_w_expert, moe_b_expert, moe_w_up, moe_w_down):
    n_ctx_batch, len_ctx, d = x_prompt.shape
    n_lat_batch, len_latent, _ = x_sample.shape
    depth = w_ada.shape[0]
    alpha = (2 * depth) ** 0.25
    n_latent_rows = n_lat_batch * len_latent
    n_ctx_rows = n_ctx_batch * len_ctx
    assert len_ctx == CONV_BLOCK_CTX and len_latent % CONV_BLOCK_LATENT == 0
    nb_latent = len_latent // CONV_BLOCK_LATENT

    x_parts = [x_sample.reshape(n_latent_rows, d), x_prompt.reshape(n_ctx_rows, d)]

    n_cond = 16
    cond = jnp.concatenate([c, c_ctx[None, :], jnp.zeros((n_cond - n_lat_batch - 1, d), F32)], axis=0)
    mod_raw = _adaln(cond, w_ada, b_ada)
    table = mod_raw.reshape(depth, n_cond, 6, d).transpose(0, 2, 1, 3).reshape(depth * 6 * n_cond, 1, d)
    mod = _Mod(table, n_cond, len_latent, n_lat_batch)

    dft_lat = [jnp.asarray(m) for m in _dft_matrices(CONV_BLOCK_LATENT)]
    dft_ctx = [jnp.asarray(m) for m in _dft_matrices(CONV_BLOCK_CTX)]

    new_k_layers, new_v_layers = [], []
    for i in range(depth):
        j = i // 2
        if i % 2 == 0:
            if len(x_parts) == 1:
                x_parts = [x_parts[0][:n_latent_rows], x_parts[0][n_latent_rows:]]
            tm = len_latent
            w_bf = hy_w_in[j].astype(BF16)
            conv = (hy_b_in[j], hy_conv_w[j], hy_conv_b[j])
            x0_l, u_l = _hyena_in(x_parts[0], mod, i, w_bf, *conv, seq_len=len_latent, row0=0, tm=tm)
            x0_c, u_c = _hyena_in(x_parts[1], mod, i, w_bf, *conv, seq_len=len_ctx, row0=n_latent_rows, tm=tm)
            filt = (hy_f_w1[j], hy_f_b1[j], hy_f_w2[j], hy_f_b2[j], hy_f_w3[j], hy_f_freq[j], hy_decay[j])
            spec_lat = _filter_spectra(nb_latent, len_latent, CONV_BLOCK_LATENT, *filt, dft_lat[0])
            spec_ctx = _filter_spectra(1, len_ctx, CONV_BLOCK_CTX, *filt, dft_ctx[0])
            a_parts = [
                _hyena_conv(u_l, x0_l, spec_lat, hy_d[j], dft_lat[0], dft_lat[1], block=CONV_BLOCK_LATENT,
                            n_seq=1, nb=nb_latent),
                _hyena_conv(u_c, x0_c, spec_ctx, hy_d[j], dft_ctx[0], dft_ctx[1], block=CONV_BLOCK_CTX,
                            n_seq=tm // len_ctx, nb=1),
            ]
            w_o, b_o = hy_w_out[j], hy_b_out[j]
        else:
            if len(x_parts) == 2:
                x_parts = [jnp.concatenate(x_parts, axis=0)]
            qkv, nk, nv = _qkv(x_parts[0], mod, i, na_w_qkv[j], na_b_qkv[j], n_latent_rows=n_latent_rows,
                               n_ctx_batch=n_ctx_batch, len_ctx=len_ctx)
            new_k_layers.append(nk)
            new_v_layers.append(nv)
            tab = _nbr_bias_tiles(na_rpb[j])
            a_parts = [
                _nbr_attn(qkv, cache_k, cache_v, tab, j, n_batch=n_lat_batch, len_latent=len_latent),
                _ctx_attn(qkv, row_block0=n_latent_rows // len_ctx, n_batch=n_ctx_batch, len_ctx=len_ctx),
            ]
            w_o, b_o = na_w_out[j], na_b_out[j]
        x, tb, route, counts = _out_proj_route(a_parts, x_parts, w_o, b_o, mod, i, ln_g[i, 0], ln_b[i, 0],
                                               alpha, moe_w_group[i], moe_b_group[i], moe_w_expert[i],
                                               moe_b_expert[i])
        split = n_latent_rows if i == depth - 1 else None
        x = _moe_layer(x, tb, route, counts, mod, i, moe_w_up, moe_w_down, ln_g[i, 1], ln_b[i, 1], alpha,
                       split_rows=split)
        x_parts = list(x) if split is not None else [x]

    y_sample = x_parts[0].reshape(n_lat_batch, len_latent, d)
    y_prompt = x_parts[1].reshape(n_ctx_batch, len_ctx, d)
    new_k = jnp.concatenate(new_k_layers, axis=1)
    new_v = jnp.concatenate(new_v_layers, axis=1)
    return (y_prompt, y_sample, new_k, new_v)
```

```python
import functools
import math

import numpy as np
import jax
import jax.numpy as jnp
from jax import lax
from jax.experimental import pallas as pl
from jax.experimental.pallas import tpu as pltpu

F32 = jnp.float32
BF16 = jnp.bfloat16
HIGHEST = lax.Precision.HIGHEST

GRID_W = 64
N_BANDS = 16
N_HEADS = 16
WIN_ROWS = 8
WIN_COLS = 16
N_GROUPS = 4
EXPERTS_PER_GROUP = 4
LN_EPS = 1e-5
LOG2_E = 1.4426950408889634

LANES = 128
SUBLANES = 8
BF16_ROWS = 16
VMEM_LIMIT_BYTES = 56 * 1024 * 1024

FILTER_PAD = 128
CONV_BLOCK_CTX = 256
CONV_BLOCK_LATENT = 512

MOE_TOKEN_TILE = 512
MOE_ROW_TILE = 512
NBR_QUERY_ROWS = 4


def _cparams(sem):
    return pltpu.CompilerParams(dimension_semantics=sem, vmem_limit_bytes=VMEM_LIMIT_BYTES)


def _dot(a, b, precision=None):
    return jnp.dot(a, b, preferred_element_type=F32, precision=precision)


def _dot_nt(a, b, precision=None):
    return lax.dot_general(a, b, (((1,), (1,)), ((), ())), preferred_element_type=F32,
                           precision=precision)


def _dot3(a, b):
    a_hi = a.astype(BF16)
    a_lo = (a - a_hi.astype(F32)).astype(BF16)
    b_hi = b.astype(BF16)
    b_lo = (b - b_hi.astype(F32)).astype(BF16)
    return _dot(a_hi, b_hi) + (_dot(a_hi, b_lo) + _dot(a_lo, b_hi))


def _silu(x):
    return x / (1.0 + jnp.exp(-x))


def _layer_norm(r, g, b):
    mu = jnp.mean(r, axis=-1, keepdims=True)
    d = r - mu
    var = jnp.mean(d * d, axis=-1, keepdims=True)
    return d * lax.rsqrt(var + LN_EPS) * g + b


def _round_up(x, m):
    return (x + m - 1) // m * m


def _adaln_body(c_ref, w_ref, b_ref, o_ref):
    o_ref[0] = _dot(_silu(c_ref[...]), w_ref[0], HIGHEST) + b_ref[0]


def _adaln(cond, w_ada, b_ada):
    depth, d, d6 = w_ada.shape
    n_cond = cond.shape[0]
    tn = 1024
    return pl.pallas_call(
        _adaln_body,
        grid=(depth, d6 // tn),
        in_specs=[
            pl.BlockSpec((n_cond, d), lambda l, j: (0, 0)),
            pl.BlockSpec((1, d, tn), lambda l, j: (l, 0, j)),
            pl.BlockSpec((1, 1, tn), lambda l, j: (l, 0, j)),
        ],
        out_specs=pl.BlockSpec((1, n_cond, tn), lambda l, j: (l, 0, j)),
        out_shape=jax.ShapeDtypeStruct((depth, n_cond, d6), F32),
        compiler_params=_cparams(("arbitrary", "arbitrary")),
        name="adaln",
    )(cond, w_ada, b_ada.reshape(depth, 1, d6))


class _Mod:
    def __init__(self, table, n_cond, rows_per_cond, n_latent_cond):
        self.table = table
        self.n_cond = n_cond
        self.rows_per_cond = rows_per_cond
        self.n_latent_cond = n_latent_cond

    def spec(self, layer, which, tm, row0=0):
        base = (layer * 6 + which) * self.n_cond
        rpc, nl = self.rows_per_cond, self.n_latent_cond
        d = self.table.shape[-1]

        def index_map(i, *_):
            return (base + jnp.minimum((row0 + i * tm) // rpc, nl), 0, 0)

        return pl.BlockSpec((1, 1, d), index_map)


def _hyena_in_body(x_ref, sh_ref, sc_ref, w0_ref, w1_ref, w2_ref, b0_ref, b1_ref, b2_ref,
                   cw0_ref, cw1_ref, cw2_ref, cb0_ref, cb1_ref, cb2_ref,
                   x0_ref, u_ref, h_ref, *, seq_len):
    j = pl.program_id(1)
    tm = x_ref.shape[0]

    @pl.when(j == 0)
    def _():
        h_ref[...] = (x_ref[...] * (1.0 + sc_ref[0]) + sh_ref[0]).astype(BF16)

    pos =lax.broadcasted_iota(jnp.int32, (tm, 1), 0) & (seq_len - 1)
    first = pos == 0
    last = pos == seq_len - 1

    def part(w_ref, b_ref, cw_ref, cb_ref):
        z = _dot(h_ref[...], w_ref[...]) + b_ref[...]
        z_prev = jnp.where(first, 0.0, pltpu.roll(z, 1, 0))
        z_next = jnp.where(last, 0.0, pltpu.roll(z, tm - 1, 0))
        cw = cw_ref[...]
        return cb_ref[...] + z_prev * cw[0:1] + z * cw[1:2] + z_next * cw[2:3]

    x0_ref[...] = part(w0_ref, b0_ref, cw0_ref, cb0_ref).astype(x0_ref.dtype)
    x1 = part(w1_ref, b1_ref, cw1_ref, cb1_ref)
    v = part(w2_ref, b2_ref, cw2_ref, cb2_ref)
    u_ref[...] = (v * x1).astype(u_ref.dtype)


def _hyena_in(x, mod, layer, w_bf, b_in, conv_w, conv_b, *, seq_len, row0, tm):
    t, d = x.shape
    tn = 256
    assert t % tm == 0 and tm % seq_len == 0 and seq_len & (seq_len - 1) == 0
    nj = d // tn
    b2d = b_in.reshape(1, 3 * d)
    cb2d = conv_b.reshape(1, 3 * d)

    def col(p):
        return lambda i, j: (0, p * nj + j)

    out = jax.ShapeDtypeStruct((t, d), BF16)
    return pl.pallas_call(
        functools.partial(_hyena_in_body, seq_len=seq_len),
        grid=(t // tm, nj),
        in_specs=[pl.BlockSpec((tm, d), lambda i, j: (i, 0)),
                  mod.spec(layer, 0, tm, row0), mod.spec(layer, 1, tm, row0)]
        + [pl.BlockSpec((d, tn), col(p)) for p in range(3)]
        + [pl.BlockSpec((1, tn), col(p)) for p in range(3)]
        + [pl.BlockSpec((3, tn), col(p)) for p in range(3)]
        + [pl.BlockSpec((1, tn), col(p)) for p in range(3)],
        out_specs=[pl.BlockSpec((tm, tn), lambda i, j: (i, j))] * 2,
        out_shape=[out, out],
        scratch_shapes=[pltpu.VMEM((tm, d), BF16)],
        compiler_params=_cparams(("arbitrary", "arbitrary")),
        name=f"hyena_in_{seq_len}",
    )(x, mod.table, mod.table, w_bf, w_bf, w_bf, b2d, b2d, b2d,
      conv_w, conv_w, conv_w, cb2d, cb2d, cb2d)


def _freq_pad(block):
    return _round_up(block + 1, SUBLANES)


def _dft_matrices(block):
    n = 2 * block
    nf = block + 1
    fp = _freq_pad(block)
    f = np.arange(nf, dtype=np.float64)[:, None]
    m = np.arange(n, dtype=np.float64)[None, :]
    ang = 2.0 * np.pi * f * m / n
    fwd = np.zeros((2 * fp, n), np.float64)
    fwd[:nf] = np.cos(ang)
    fwd[fp:fp + nf] = -np.sin(ang)
    a = np.arange(block, dtype=np.float64)[:, None]
    fr = np.arange(nf, dtype=np.float64)[None, :]
    ang_i = 2.0 * np.pi * a * fr / n
    weight = np.full((1, nf), 2.0)
    weight[0, 0] = 1.0
    weight[0, nf - 1] = 1.0
    inv = np.zeros((block, 2 * fp), np.float64)
    inv[:, :nf] = weight * np.cos(ang_i) / n
    inv[:, fp:fp + nf] = -weight * np.sin(ang_i) / n
    return fwd.astype(np.float32), inv.astype(np.float32)


def _lag_tables(nb, seq_len, block):
    n = 2 * block
    lag = np.zeros((2 * nb - 1, n, 1), np.float32)
    valid = np.zeros((2 * nb - 1, n, 1), np.float32)
    m = np.arange(n)
    for dd in range(2 * nb - 1):
        delta = dd - (nb - 1)
        lg = np.where(m < block, block * delta + m, block * delta + m - n)
        ok = (m != block) & (np.abs(lg) <= seq_len - 1)
        lag[dd, :, 0] = np.where(ok, lg, 0)
        valid[dd, :, 0] = ok
    return lag, valid


def _filter_body(lag_ref, valid_ref, bands_ref, w1t_ref, w1c_ref, w1s_ref, b1_ref, w2_ref, b2_ref,
                 freq_ref, w3_ref, dec_ref, dft_ref, k_ref, *, seq_len):
    d = k_ref.shape[-1]
    lag = lag_ref[0]
    pos = jnp.abs(lag)
    tt = pos / float(max(seq_len - 1, 1))
    ang = (2.0 * math.pi / seq_len) * pos * bands_ref[...]
    freq = freq_ref[...]
    pre = (tt * w1t_ref[...] + _dot(jnp.cos(ang), w1c_ref[...], HIGHEST)
           + _dot(-jnp.sin(ang), w1s_ref[...], HIGHEST) + b1_ref[...])
    h = jnp.sin(freq * pre)
    h = jnp.sin(freq * (_dot(h, w2_ref[...], HIGHEST) + b2_ref[...]))
    hf = _dot3(h, w3_ref[:, :d]) * jnp.exp(-tt * jnp.abs(dec_ref[:, :d]))
    hb = _dot3(h, w3_ref[:, d:]) * jnp.exp(-tt * jnp.abs(dec_ref[:, d:]))
    taps = jnp.where(valid_ref[0] > 0.5, jnp.where(lag >= 0.0, hf, hb), 0.0)
    k_ref[0] = _dot3(dft_ref[...], taps)


def _filter_spectra(nb, seq_len, block, f_w1, f_b1, f_w2, f_b2, f_w3, f_freq, decay, dft_fwd):
    hid = f_w1.shape[1]
    d2 = f_w3.shape[1]
    d = d2 // 2
    n = 2 * block
    fp2 = dft_fwd.shape[0]
    lag, valid = _lag_tables(nb, seq_len, block)
    ph = FILTER_PAD - hid
    bands = np.zeros((1, LANES), np.float32)
    bands[0, :N_BANDS] = np.linspace(1e-4, N_BANDS - 1, N_BANDS, dtype=np.float32)
    w1t = jnp.pad(f_w1[0:1], ((0, 0), (0, ph)))
    w1c = jnp.pad(f_w1[1:1 + N_BANDS], ((0, LANES - N_BANDS), (0, ph)))
    w1s = jnp.pad(f_w1[1 + N_BANDS:1 + 2 * N_BANDS], ((0, LANES - N_BANDS), (0, ph)))
    b1 = jnp.pad(f_b1.reshape(1, hid), ((0, 0), (0, ph)))
    w2 = jnp.pad(f_w2, ((0, ph), (0, ph)))
    b2 = jnp.pad(f_b2.reshape(1, hid), ((0, 0), (0, ph)))
    freq = jnp.pad(f_freq.reshape(1, hid), ((0, 0), (0, ph)))
    w3 = jnp.pad(f_w3, ((0, ph), (0, 0)))
    dec = decay.reshape(1, d2)
    nd = 2 * nb - 1

    def whole(shape):
        return pl.BlockSpec(shape, lambda dd: (0,) * len(shape))

    return pl.pallas_call(
        functools.partial(_filter_body, seq_len=seq_len),
        grid=(nd,),
        in_specs=[
            pl.BlockSpec((1, n, 1), lambda dd: (dd, 0, 0)),
            pl.BlockSpec((1, n, 1), lambda dd: (dd, 0, 0)),
            whole((1, LANES)), whole((1, FILTER_PAD)), whole((LANES, FILTER_PAD)),
            whole((LANES, FILTER_PAD)), whole((1, FILTER_PAD)), whole((FILTER_PAD, FILTER_PAD)),
            whole((1, FILTER_PAD)), whole((1, FILTER_PAD)),
            whole((FILTER_PAD, d2)), whole((1, d2)), whole((fp2, n)),
        ],
        out_specs=pl.BlockSpec((1, fp2, d), lambda dd: (dd, 0, 0)),
        out_shape=jax.ShapeDtypeStruct((nd, fp2, d), F32),
        compiler_params=_cparams(("arbitrary",)),
        name=f"hyena_filter_{seq_len}",
    )(jnp.asarray(lag), jnp.asarray(valid), jnp.asarray(bands), w1t, w1c, w1s, b1, w2, b2, freq,
      w3, dec, dft_fwd)


def _hyena_conv_body(u_ref, x0_ref, k_ref, dsk_ref, dftu_ref, idft_ref, o_ref, uf_ref, yf_ref,
                     *, n_seq, nb, block):
    fp = k_ref.shape[1] // 2
    ft_rows = SUBLANES

    def fwd(blk, carry):
        rows = pl.ds(pl.multiple_of(blk * block, block), block)
        uf_ref[blk] = _dot(dftu_ref[...], u_ref[rows, :])
        return carry

    lax.fori_loop(0, n_seq * nb, fwd, 0)

    def freq_tile(ft, carry):
        r = pl.multiple_of(ft * ft_rows, ft_rows)
        re = pl.ds(r, ft_rows)
        im = pl.ds(fp + r, ft_rows)
        k_re = [k_ref[dd, re, :] for dd in range(2 * nb - 1)]
        k_im = [k_ref[dd, im, :] for dd in range(2 * nb - 1)]
        for s in range(n_seq):
            u_re = [uf_ref[s * nb + j, re, :] for j in range(nb)]
            u_im = [uf_ref[s * nb + j, im, :] for j in range(nb)]
            for i in range(nb):
                acc_re = None
                acc_im = None
                for j in range(nb):
                    dd = i - j + (nb - 1)
                    t_re = k_re[dd] * u_re[j] - k_im[dd] * u_im[j]
                    t_im = k_re[dd] * u_im[j] + k_im[dd] * u_re[j]
                    acc_re = t_re if acc_re is None else acc_re + t_re
                    acc_im = t_im if acc_im is None else acc_im + t_im
                yf_ref[s * nb + i, re, :] = acc_re
                yf_ref[s * nb + i, im, :] = acc_im
        return carry

    lax.fori_loop(0, fp // ft_rows, freq_tile, 0)

    def inv(blk, carry):
        rows = pl.ds(pl.multiple_of(blk * block, block), block)
        y = _dot(idft_ref[...], yf_ref[blk].astype(BF16))
        u_blk = u_ref[rows, :].astype(F32)
        o_ref[rows, :] = ((y + u_blk * dsk_ref[...]) * x0_ref[rows, :].astype(F32)).astype(o_ref.dtype)
        return carry

    lax.fori_loop(0, n_seq * nb, inv, 0)


def _hyena_conv(u, x0, spectra, d_skip, dft_fwd, dft_inv, *, block, n_seq, nb):
    t, d = u.shape
    ct = 256
    tm = n_seq * nb * block
    nd = 2 * nb - 1
    fp2 = dft_fwd.shape[0]
    rows = lambda c, b: (b, c)
    return pl.pallas_call(
        functools.partial(_hyena_conv_body, n_seq=n_seq, nb=nb, block=block),
        grid=(d // ct, t // tm),
        in_specs=[
            pl.BlockSpec((tm, ct), rows),
            pl.BlockSpec((tm, ct), rows),
            pl.BlockSpec((nd, fp2, ct), lambda c, b: (0, 0, c)),
            pl.BlockSpec((1, ct), lambda c, b: (0, c)),
            pl.BlockSpec((fp2, block), lambda c, b: (0, 0)),
            pl.BlockSpec((block, fp2), lambda c, b: (0, 0)),
        ],
        out_specs=pl.BlockSpec((tm, ct), rows),
        out_shape=jax.ShapeDtypeStruct((t, d), BF16),
        scratch_shapes=[pltpu.VMEM((n_seq * nb, fp2, ct), F32),
                        pltpu.VMEM((n_seq * nb, fp2, ct), F32)],
        compiler_params=_cparams(("arbitrary", "arbitrary")),
        name=f"hyena_conv_nb{nb}",
    )(u, x0, spectra, d_skip.reshape(1, d), dft_fwd[:, :block].astype(BF16), dft_inv.astype(BF16))


def _out_proj_body(*refs, alpha, split_tile, x_parts):
    a_refs = refs[0:2]
    x_refs = refs[2:2 + x_parts]
    (w_ref, b_ref, gate_ref, g_ref, beta_ref, sh_ref, sc_ref, rw_ref, rb_ref,
     o_ref, t_ref, route_ref, cnt_ref) = refs[2 + x_parts:]
    i = pl.program_id(0)

    def run(a_ref, x_ref):
        o = _dot(a_ref[...], w_ref[...]) + b_ref[...]
        r = alpha * x_ref[...] + gate_ref[0] * o
        o_ref[...] = _layer_norm(r, g_ref[...], beta_ref[...])

    @pl.when(i < split_tile)
    def _():
        run(a_refs[0], x_refs[0])

    @pl.when(i >= split_tile)
    def _():
        run(a_refs[1], x_refs[-1])

    t = o_ref[...] * (1.0 + sc_ref[0]) + sh_ref[0]
    t_ref[...] = t.astype(BF16)
    route, counts = _route_tokens(t, rw_ref[...], rb_ref[...])
    route_ref[...] = route
    cnt_ref[0] = jnp.broadcast_to(counts, cnt_ref.shape[1:])


def _out_proj_route(a_parts, x_parts, w, b, mod, layer, ln_g, ln_b, alpha, w_group, b_group, w_expert,
                    b_expert):
    d = a_parts[0].shape[1]
    tm = MOE_TOKEN_TILE
    split_tile = a_parts[0].shape[0] // tm
    t = a_parts[0].shape[0] + a_parts[1].shape[0]
    pad = LANES - N_GROUPS - w_expert.shape[1]
    rw = jnp.pad(jnp.concatenate([w_group, w_expert], axis=1), ((0, 0), (0, pad)))
    rb = jnp.pad(jnp.concatenate([b_group, b_expert]), (0, pad)).reshape(1, LANES)
    first = lambda i: (jnp.minimum(i, split_tile - 1), 0)
    second = lambda i: (jnp.maximum(i - split_tile, 0), 0)
    row = lambda i: (i, 0)
    const = lambda i: (0, 0)
    part_specs = [pl.BlockSpec((tm, d), first), pl.BlockSpec((tm, d), second)]
    x_specs = part_specs if len(x_parts) == 2 else [pl.BlockSpec((tm, d), row)]
    return pl.pallas_call(
        functools.partial(_out_proj_body, alpha=alpha, split_tile=split_tile, x_parts=len(x_parts)),
        grid=(t // tm,),
        in_specs=part_specs + x_specs
        + [pl.BlockSpec((d, d), const), pl.BlockSpec((1, d), const), mod.spec(layer, 2, tm),
           pl.BlockSpec((1, d), const), pl.BlockSpec((1, d), const),
           mod.spec(layer, 3, tm), mod.spec(layer, 4, tm),
           pl.BlockSpec((d, LANES), const), pl.BlockSpec((1, LANES), const)],
        out_specs=[pl.BlockSpec((tm, d), row), pl.BlockSpec((tm, d), row), pl.BlockSpec((tm, LANES), row),
                   pl.BlockSpec((1, SUBLANES, LANES), lambda i: (i, 0, 0))],
        out_shape=[jax.ShapeDtypeStruct((t, d), F32), jax.ShapeDtypeStruct((t, d), BF16),
                   jax.ShapeDtypeStruct((t, LANES), F32),
                   jax.ShapeDtypeStruct((t // tm, SUBLANES, LANES), F32)],
        compiler_params=_cparams(("arbitrary",)),
        name="out_proj_ln_route",
    )(*a_parts, *x_parts, w.astype(BF16), b.reshape(1, d), mod.table,
      ln_g.reshape(1, d), ln_b.reshape(1, d), mod.table, mod.table, rw, rb)


def _route_tokens(t, w, b):
    tm = t.shape[0]
    lg = _dot3(t, w) + b
    lane = lax.broadcasted_iota(jnp.int32, lg.shape, 1)
    neg = -jnp.inf
    far = jnp.int32(LANES)
    g_mask = lane < N_GROUPS
    gl = jnp.where(g_mask, lg, neg)
    g_max = jnp.max(gl, axis=-1, keepdims=True)
    g_sel = jnp.min(jnp.where(gl == g_max, lane, far), axis=-1, keepdims=True)
    g_w = 1.0 / jnp.sum(jnp.where(g_mask, jnp.exp(lg - g_max), 0.0), axis=-1, keepdims=True)
    e_lo = N_GROUPS + EXPERTS_PER_GROUP * g_sel
    e_mask = (lane >= e_lo) & (lane < e_lo + EXPERTS_PER_GROUP)
    el = jnp.where(e_mask, lg, neg)
    m1 = jnp.max(el, axis=-1, keepdims=True)
    i1 = jnp.min(jnp.where(el == m1, lane, far), axis=-1, keepdims=True)
    el2 = jnp.where(lane == i1, neg, el)
    m2 = jnp.max(el2, axis=-1, keepdims=True)
    i2 = jnp.min(jnp.where(el2 == m2, lane, far), axis=-1, keepdims=True)
    ratio = jnp.exp(m2 - m1)
    w1 = 1.0 / (1.0 + ratio)
    w2 = ratio / (1.0 + ratio)
    e1 = i1 - N_GROUPS
    e2 = i2 - N_GROUPS
    hit = (lane == e1) | (lane == e2)
    counts = jnp.sum(hit.astype(F32), axis=0, keepdims=True)
    run_len = jnp.floor((counts + (BF16_ROWS - 1)) * (1.0 / BF16_ROWS)) * BF16_ROWS
    lower = (lax.broadcasted_iota(jnp.int32, (LANES, LANES), 0)
             < lax.broadcasted_iota(jnp.int32, (LANES, LANES), 1)).astype(F32)
    starts = _dot(jnp.broadcast_to(run_len, (SUBLANES, LANES)), lower, HIGHEST)[0:1]
    earlier = (lax.broadcasted_iota(jnp.int32, (tm, tm), 1)
               < lax.broadcasted_iota(jnp.int32, (tm, tm), 0)).astype(BF16)
    pos = _dot(earlier, hit.astype(BF16)) + starts
    p1 = jnp.sum(jnp.where(lane == e1, pos, 0.0), axis=-1, keepdims=True)
    p2 = jnp.sum(jnp.where(lane == e2, pos, 0.0), axis=-1, keepdims=True)
    fields = (e1.astype(F32), e2.astype(F32), g_w * w1, g_w * w2, p1, p2)
    route = jnp.zeros(lg.shape, F32)
    for k, val in enumerate(fields):
        route = jnp.where(lane == k, val, route)
    return route, counts


def _moe_layout(counts, n_experts, n_row_tiles_max, chunk_slots):
    cnt = counts[:, 0, :n_experts].astype(jnp.int32)
    run_len = (cnt + (BF16_ROWS - 1)) // BF16_ROWS * BF16_ROWS
    total = jnp.sum(run_len, axis=0)
    region = (total + (MOE_ROW_TILE - 1)) // MOE_ROW_TILE * MOE_ROW_TILE
    region_end = jnp.cumsum(region)
    expert_start = region_end - region
    run_start = expert_start[None, :] + jnp.cumsum(run_len, axis=0) - run_len
    n_used = (region_end[-1] // MOE_ROW_TILE).reshape(1).astype(jnp.int32)
    tile_idx = jnp.arange(n_row_tiles_max, dtype=jnp.int32)
    tile_expert = jnp.minimum(
        jnp.sum((tile_idx[:, None] >= (region_end // MOE_ROW_TILE)[None, :]).astype(jnp.int32), axis=1),
        n_experts - 1).astype(jnp.int32)
    used_rows = region_end[-1:]
    tail_chunks = (n_row_tiles_max * MOE_ROW_TILE - used_rows) // (MOE_ROW_TILE // 2)
    local_end = jnp.cumsum(run_len, axis=1)
    pos = jnp.arange(chunk_slots, dtype=jnp.int32) * BF16_ROWS
    run_of_chunk = jnp.sum((pos[None, :, None] >= local_end[:, None, :]).astype(jnp.int32), axis=-1)
    in_run = run_of_chunk[:, :, None] == jnp.arange(n_experts, dtype=jnp.int32)[None, None, :]
    shift = (run_start - (local_end - run_len))[:, None, :]
    chunk_row = jnp.sum(jnp.where(in_run, shift, 0), axis=-1) + pos[None, :]
    chunk_row = jnp.where(pos[None, :] < local_end[:, -1:], chunk_row, 0)
    return dict(chunk_row=chunk_row.reshape(-1).astype(jnp.int32),
                n_chunks=(local_end[:, -1] // BF16_ROWS).astype(jnp.int32),
                gap_start=jnp.concatenate([expert_start + total, used_rows]).astype(jnp.int32),
                gap_len=jnp.concatenate([region - total, tail_chunks]).astype(jnp.int32),
                tile_expert=tile_expert, n_used=n_used)


def _run_chunk_loops(chunk_copy, n_chunks):
    def start(c, carry):
        chunk_copy(c).start()
        return carry

    def wait(c, carry):
        chunk_copy(c).wait()
        return carry

    return (lambda: lax.fori_loop(0, n_chunks, start, 0)), (lambda: lax.fori_loop(0, n_chunks, wait, 0))


def _chunk_copies(src_ref, src_off, dst_ref, dst_off, n, sem, max_chunk, advance_src=True):
    out = []
    off = jnp.int32(0)
    bit = max_chunk
    while bit >= BF16_ROWS:
        take = n & bit
        s = pl.multiple_of(src_off + off, BF16_ROWS) if advance_src else src_off
        t = pl.multiple_of(dst_off + off, BF16_ROWS)
        cp = pltpu.make_async_copy(src_ref.at[pl.ds(s, bit)], dst_ref.at[pl.ds(t, bit)], sem)
        out.append((take != 0, cp))
        off = off + take
        bit //= 2
    return out


def _start_all(copies):
    for cond, cp in copies:
        pl.when(cond)(cp.start)


def _wait_all(copies):
    for cond, cp in copies:
        pl.when(cond)(cp.wait)


def _dispatch_body(cd_ref, nch_ref, gs_ref, gl_ref, t_ref, route_ref, xg_ref, sorted_ref, zero_ref, sems,
                   *, n_experts):
    tile = pl.program_id(0)
    last_tile = pl.num_programs(0) - 1
    buf = lax.rem(tile, 2)
    sem = sems.at[0]
    tm = t_ref.shape[0]
    n_rows = sorted_ref.shape[1]
    slots = n_rows // BF16_ROWS
    route = route_ref[...]
    lane = lax.broadcasted_iota(jnp.int32, (tm, LANES), 1)
    both = jnp.where(lane == 0, route[:, 4:5], jnp.where(lane == 1, route[:, 5:6], 0.0))
    pick = (lax.broadcasted_iota(jnp.int32, (SUBLANES, LANES), 0)
            == lax.broadcasted_iota(jnp.int32, (SUBLANES, LANES), 1)).astype(F32)
    as_rows = _dot_nt(pick, both, HIGHEST)
    r_iota = lax.broadcasted_iota(jnp.int32, (n_rows, tm), 0).astype(F32)
    select = ((r_iota == as_rows[0:1]) | (r_iota == as_rows[1:2])).astype(BF16)
    sorted_ref[buf] = _dot(select, t_ref[...]).astype(BF16)

    def chunk_loops(tl, b):
        def chunk_copy(c):
            src = pl.multiple_of(c * BF16_ROWS, BF16_ROWS)
            dst = pl.multiple_of(cd_ref[tl * slots + c], BF16_ROWS)
            return pltpu.make_async_copy(sorted_ref.at[b, pl.ds(src, BF16_ROWS)],
                                         xg_ref.at[pl.ds(dst, BF16_ROWS)], sems.at[b])

        return _run_chunk_loops(chunk_copy, nch_ref[tl])

    start_chunks, wait_chunks = chunk_loops(tile, buf)
    start_chunks()

    @pl.when(tile > 0)
    def _():
        chunk_loops(tile - 1, 1 - buf)[1]()

    @pl.when(tile == last_tile)
    def _():
        wait_chunks()
        zero_ref[...] = jnp.zeros_like(zero_ref)
        fills = []
        for e in range(n_experts):
            fills += _chunk_copies(zero_ref, 0, xg_ref, gs_ref[e], gl_ref[e], sem, zero_ref.shape[0],
                                   advance_src=False)
        _start_all(fills)
        _wait_all(fills)
        z_rows = zero_ref.shape[0]
        tail0 = gs_ref[n_experts]

        def tail_copy(c):
            dst = pl.multiple_of(tail0 + c * z_rows, z_rows)
            return pltpu.make_async_copy(zero_ref, xg_ref.at[pl.ds(dst, z_rows)], sem)

        def start(c, carry):
            tail_copy(c).start()
            return carry

        def wait(c, carry):
            tail_copy(c).wait()
            return carry

        lax.fori_loop(0, gl_ref[n_experts], start, 0)
        lax.fori_loop(0, gl_ref[n_experts], wait, 0)


def _sorted_rows(tm, n_experts):
    return _round_up(2 * tm + n_experts * (BF16_ROWS - 1), LANES)


def _dispatch(tb, route, layout, n_experts, n_rows_max):
    t, d = tb.shape
    tm = MOE_TOKEN_TILE
    row = lambda i, *_: (i, 0)
    grid_spec = pltpu.PrefetchScalarGridSpec(
        num_scalar_prefetch=4,
        grid=(t // tm,),
        in_specs=[pl.BlockSpec((tm, d), row), pl.BlockSpec((tm, LANES), row)],
        out_specs=pl.BlockSpec(memory_space=pl.ANY),
        scratch_shapes=[pltpu.VMEM((2, _sorted_rows(tm, n_experts), d), BF16),
                        pltpu.VMEM((MOE_ROW_TILE // 2, d), BF16),
                        pltpu.SemaphoreType.DMA((2,))],
    )
    return pl.pallas_call(
        functools.partial(_dispatch_body, n_experts=n_experts),
        grid_spec=grid_spec,
        out_shape=jax.ShapeDtypeStruct((n_rows_max, d), BF16),
        compiler_params=_cparams(("arbitrary",)),
        name="moe_dispatch",
    )(layout["chunk_row"], layout["n_chunks"], layout["gap_start"], layout["gap_len"], tb, route)


def _experts_body(te_ref, nu_ref, x_ref, wu_ref, wd_ref, y_ref, wub_ref, wdb_ref, *, d_expert):
    g = pl.program_id(0)

    @pl.when(g < nu_ref[0])
    def _():
        prev = te_ref[jnp.maximum(g - 1, 0)]

        @pl.when((g == 0) | (te_ref[g] != prev))
        def _():
            wub_ref[...] = wu_ref[0, 0].astype(BF16)
            wdb_ref[...] = wd_ref[0, 0].astype(BF16)

        ab = _dot(x_ref[...], wub_ref[...])
        h = _silu(ab[:, :d_expert]) * ab[:, d_expert:]
        y_ref[...] = _dot(h.astype(BF16), wdb_ref[...]).astype(y_ref.dtype)

    @pl.when(g >= nu_ref[0])
    def _():
        y_ref[...] = jnp.zeros_like(y_ref)


def _experts(xg, w_up, w_down, layer, layout):
    n_rows, d = xg.shape
    d_up = w_up.shape[-1]
    d_expert = d_up // 2
    tm = MOE_ROW_TILE
    used = lambda g, nu: jnp.maximum(jnp.minimum(g, nu[0] - 1), 0)
    grid_spec = pltpu.PrefetchScalarGridSpec(
        num_scalar_prefetch=2,
        grid=(n_rows // tm,),
        in_specs=[pl.BlockSpec((tm, d), lambda g, te, nu: (used(g, nu), 0)),
                  pl.BlockSpec((1, 1, d, d_up), lambda g, te, nu: (layer, te[used(g, nu)], 0, 0)),
                  pl.BlockSpec((1, 1, d_expert, d), lambda g, te, nu: (layer, te[used(g, nu)], 0, 0))],
        out_specs=pl.BlockSpec((tm, d), lambda g, te, nu: (g, 0)),
        scratch_shapes=[pltpu.VMEM((d, d_up), BF16), pltpu.VMEM((d_expert, d), BF16)],
    )
    return pl.pallas_call(
        functools.partial(_experts_body, d_expert=d_expert),
        grid_spec=grid_spec,
        out_shape=jax.ShapeDtypeStruct((n_rows, d), BF16),
        compiler_params=_cparams(("arbitrary",)),
        name="moe_experts",
    )(layout["tile_expert"], layout["n_used"], xg, w_up, w_down)


def _combine_body(cd_ref, nch_ref, yg_ref, route_ref, x_ref, gate_ref, g_ref, beta_ref, *rest,
                  alpha, split_tile):
    if split_tile is None:
        o_ref, ybuf_ref, sems = rest
    else:
        o_ref, o2_ref, ybuf_ref, sems = rest
    tile = pl.program_id(0)
    buf = lax.rem(tile, 2)
    tm = x_ref.shape[0]
    n_rows = ybuf_ref.shape[1]
    slots = n_rows // BF16_ROWS

    def chunk_loops(tl, b):
        def chunk_copy(c):
            src = pl.multiple_of(cd_ref[tl * slots + c], BF16_ROWS)
            dst = pl.multiple_of(c * BF16_ROWS, BF16_ROWS)
            return pltpu.make_async_copy(yg_ref.at[pl.ds(src, BF16_ROWS)],
                                         ybuf_ref.at[b, pl.ds(dst, BF16_ROWS)], sems.at[b])

        return _run_chunk_loops(chunk_copy, nch_ref[tl])

    def fetch(tl, b):
        ybuf_ref[b, pl.ds(2 * tm, n_rows - 2 * tm), :] = jnp.zeros(
            (n_rows - 2 * tm, ybuf_ref.shape[2]), ybuf_ref.dtype)
        chunk_loops(tl, b)[0]()

    @pl.when(tile == 0)
    def _():
        fetch(tile, buf)

    @pl.when(tile + 1 < pl.num_programs(0))
    def _():
        fetch(tile + 1, 1 - buf)

    route = route_ref[...]
    r_lane = lax.broadcasted_iota(jnp.int32, (tm, n_rows), 1).astype(F32)
    cmat = (jnp.where(r_lane == route[:, 4:5], route[:, 2:3], 0.0)
            + jnp.where(r_lane == route[:, 5:6], route[:, 3:4], 0.0)).astype(BF16)
    chunk_loops(tile, buf)[1]()
    y = _dot(cmat, ybuf_ref[buf])
    r = alpha * x_ref[...] + gate_ref[0] * y
    res = _layer_norm(r, g_ref[...], beta_ref[...])
    if split_tile is None:
        o_ref[...] = res
    else:
        @pl.when(tile < split_tile)
        def _():
            o_ref[...] = res

        @pl.when(tile >= split_tile)
        def _():
            o2_ref[...] = res


def _combine(yg, route, x, layout, mod, layer, ln_g, ln_b, alpha, n_experts, split_rows=None):
    t, d = x.shape
    tm = MOE_TOKEN_TILE
    row = lambda i, *_: (i, 0)
    const = lambda i, *_: (0, 0)
    if split_rows is None:
        split_tile = None
        out_specs = pl.BlockSpec((tm, d), row)
        out_shape = jax.ShapeDtypeStruct((t, d), F32)
    else:
        assert split_rows % tm == 0
        split_tile = split_rows // tm
        out_specs = [pl.BlockSpec((tm, d), lambda i, *_: (jnp.minimum(i, split_tile - 1), 0)),
                     pl.BlockSpec((tm, d), lambda i, *_: (jnp.maximum(i - split_tile, 0), 0))]
        out_shape = [jax.ShapeDtypeStruct((split_rows, d), F32),
                     jax.ShapeDtypeStruct((t - split_rows, d), F32)]
    grid_spec = pltpu.PrefetchScalarGridSpec(
        num_scalar_prefetch=2,
        grid=(t // tm,),
        in_specs=[pl.BlockSpec(memory_space=pl.ANY), pl.BlockSpec((tm, LANES), row),
                  pl.BlockSpec((tm, d), row), mod.spec(layer, 5, tm),
                  pl.BlockSpec((1, d), const), pl.BlockSpec((1, d), const)],
        out_specs=out_specs,
        scratch_shapes=[pltpu.VMEM((2, _sorted_rows(tm, n_experts), d), BF16),
                        pltpu.SemaphoreType.DMA((2,))],
    )
    return pl.pallas_call(
        functools.partial(_combine_body, alpha=alpha, split_tile=split_tile),
        grid_spec=grid_spec,
        out_shape=out_shape,
        compiler_params=_cparams(("arbitrary",)),
        name="moe_combine_ln",
    )(layout["chunk_row"], layout["n_chunks"], yg, route, x, mod.table,
      ln_g.reshape(1, d), ln_b.reshape(1, d))


def _moe_layer(x, tb, route, counts, mod, layer, w_up, w_down, ln_g, ln_b, alpha, split_rows=None):
    t, d = x.shape
    n_e = w_up.shape[1]
    n_tok_tiles = t // MOE_TOKEN_TILE
    max_rows = 2 * t + n_tok_tiles * n_e * (BF16_ROWS - 1) + n_e * (MOE_ROW_TILE - BF16_ROWS)
    n_rows_max = _round_up(max_rows, MOE_ROW_TILE)
    layout = _moe_layout(counts, n_e, n_rows_max // MOE_ROW_TILE,
                         _sorted_rows(MOE_TOKEN_TILE, n_e) // BF16_ROWS)
    xg = _dispatch(tb, route, layout, n_e, n_rows_max)
    yg = _experts(xg, w_up, w_down, layer, layout)
    return _combine(yg, route, x, layout, mod, layer, ln_g, ln_b, alpha, n_e, split_rows)


def _qkv_body(y_ref, sh_ref, sc_ref, w_ref, b_ref, qkv_ref, nk_ref, nv_ref, *, n_latent_tiles, d):
    i = pl.program_id(0)
    h = (y_ref[...] * (1.0 + sc_ref[0]) + sh_ref[0]).astype(BF16)
    z = _dot(h, w_ref[...]) + b_ref[...]
    qkv_ref[...] = z.astype(BF16)

    @pl.when(i >= n_latent_tiles)
    def _():
        hd = nk_ref.shape[-1]
        for hh in range(nk_ref.shape[2]):
            nk_ref[0, 0, hh] = z[:, d + hh * hd:d + (hh + 1) * hd]
            nv_ref[0, 0, hh] = z[:, 2 * d + hh * hd:2 * d + (hh + 1) * hd]


def _qkv(y, mod, layer, w, b, *, n_latent_rows, n_ctx_batch, len_ctx):
    t, d = y.shape
    tm = len_ctx
    hd = d // N_HEADS
    n_lat = n_latent_rows // tm
    kv_map = lambda i: (jnp.maximum(i - n_lat, 0), 0, 0, 0, 0)
    kv_shape = jax.ShapeDtypeStruct((n_ctx_batch, 1, N_HEADS, len_ctx, hd), F32)
    return pl.pallas_call(
        functools.partial(_qkv_body, n_latent_tiles=n_lat, d=d),
        grid=(t // tm,),
        in_specs=[pl.BlockSpec((tm, d), lambda i: (i, 0)), mod.spec(layer, 0, tm), mod.spec(layer, 1, tm),
                  pl.BlockSpec((d, 3 * d), lambda i: (0, 0)), pl.BlockSpec((1, 3 * d), lambda i: (0, 0))],
        out_specs=[pl.BlockSpec((tm, 3 * d), lambda i: (i, 0)),
                   pl.BlockSpec((1, 1, N_HEADS, len_ctx, hd), kv_map),
                   pl.BlockSpec((1, 1, N_HEADS, len_ctx, hd), kv_map)],
        out_shape=[jax.ShapeDtypeStruct((t, 3 * d), BF16), kv_shape, kv_shape],
        compiler_params=_cparams(("arbitrary",)),
        name="attn_qkv",
    )(y, mod.table, mod.table, w.astype(BF16), b.reshape(1, 3 * d))


def _ctx_attn_body(qkv_ref, o_ref, *, d, scale):
    pair = 2 * (d // N_HEADS)
    left = lax.broadcasted_iota(jnp.int32, (1, pair), 1) < pair // 2
    c = scale * LOG2_E
    for hp in range(d // pair):
        cols = slice(hp * pair, (hp + 1) * pair)
        q2 = qkv_ref[:, cols]
        k2 = qkv_ref[:, d + hp * pair:d + (hp + 1) * pair]
        v2 = qkv_ref[:, 2 * d + hp * pair:2 * d + (hp + 1) * pair]
        outs = []
        for mask in (left, ~left):
            s = _dot_nt(jnp.where(mask, q2, jnp.zeros_like(q2)), k2) * c
            e = jnp.exp2(s - jnp.max(s, axis=-1, keepdims=True))
            p = (e / jnp.sum(e, axis=-1, keepdims=True)).astype(BF16)
            outs.append(_dot(p, v2))
        o_ref[:, cols] = jnp.where(left, outs[0], outs[1]).astype(o_ref.dtype)


def _ctx_attn(qkv, *, row_block0, n_batch, len_ctx):
    d3 = qkv.shape[1]
    d = d3 // 3
    scale = (d // N_HEADS) ** -0.5
    return pl.pallas_call(
        functools.partial(_ctx_attn_body, d=d, scale=scale),
        grid=(n_batch,),
        in_specs=[pl.BlockSpec((len_ctx, d3), lambda b: (row_block0 + b, 0))],
        out_specs=pl.BlockSpec((len_ctx, d), lambda b: (b, 0)),
        out_shape=jax.ShapeDtypeStruct((n_batch * len_ctx, d), BF16),
        compiler_params=_cparams(("arbitrary",)),
        name="ctx_attention",
    )(qkv)


def _nbr_bias_body(rpb_ref, onehot_ref, band_ref, o_ref):
    picked = _dot(rpb_ref[...], onehot_ref[...], HIGHEST)
    o_ref[...] = jnp.where(band_ref[...] > 0.5, picked * LOG2_E, -jnp.inf)


def _nbr_bias_tiles(rpb):
    n_h, n_dr, n_dc = rpb.shape
    col = np.arange(GRID_W)
    col_start = np.clip(col - WIN_COLS // 2, 0, GRID_W - WIN_COLS)
    in_band = (col[None, :] >= col_start[:, None]) & (col[None, :] < col_start[:, None] + WIN_COLS)
    dc_idx = np.clip(col[None, :] - col[:, None], -(WIN_COLS - 1), WIN_COLS - 1) + (WIN_COLS - 1)
    n_pairs = GRID_W * GRID_W
    onehot = np.zeros((LANES, n_pairs), np.float32)
    onehot[dc_idx.reshape(-1), np.arange(n_pairs)] = 1.0
    band = in_band.reshape(1, n_pairs).astype(np.float32)
    rows = n_h * n_dr
    rpb2d = jnp.pad(rpb.reshape(rows, n_dc), ((0, 0), (0, LANES - n_dc)))
    whole = lambda shape: pl.BlockSpec(shape, lambda i: (0,) * len(shape))
    cols = pl.pallas_call(
        _nbr_bias_body,
        grid=(1,),
        in_specs=[whole((rows, LANES)), whole((LANES, n_pairs)), whole((1, n_pairs))],
        out_specs=whole((rows, n_pairs)),
        out_shape=jax.ShapeDtypeStruct((rows, n_pairs), F32),
        compiler_params=_cparams(("arbitrary",)),
        name="nbr_bias",
    )(rpb2d, jnp.asarray(onehot), jnp.asarray(band))
    cols = cols.reshape(n_h, n_dr, GRID_W, GRID_W)
    n_entries = 2 * WIN_ROWS
    masked = jnp.full((n_h, n_entries - n_dr, GRID_W, GRID_W), -jnp.inf, F32)
    cols = jnp.concatenate([cols, masked], axis=1)
    zeros = jnp.zeros_like(cols)
    left = jnp.concatenate([cols, zeros], axis=-1)
    right = jnp.concatenate([zeros, cols], axis=-1)
    return jnp.concatenate([left, right], axis=1)


def _nbr_attn_body(q_ref, k_ref, v_ref, ck_ref, cv_ref, tab_ref, o_ref, *, rows, kh, scale):
    hd = ck_ref.shape[-1]
    qr = NBR_QUERY_ROWS
    span = qr + kh
    n_q = qr * GRID_W
    n_k = span * GRID_W
    masked = 2 * WIN_ROWS - 1
    c = scale * LOG2_E
    left = lax.broadcasted_iota(jnp.int32, (1, 2 * hd), 1) < hd
    kc2 = jnp.concatenate([ck_ref[0, 0, 0], ck_ref[0, 0, 1]], axis=1).astype(BF16)
    vc2 = jnp.concatenate([cv_ref[0, 0, 0], cv_ref[0, 0, 1]], axis=1).astype(BF16)

    def block(qb):
        r0 = qb * qr
        s0 = jnp.clip(r0 - kh // 2, 0, rows - span)
        q_rows = pl.ds(pl.multiple_of(r0 * GRID_W, n_q), n_q)
        k_rows = pl.ds(pl.multiple_of(s0 * GRID_W, GRID_W), n_k)
        q2 = q_ref[q_rows, :]
        kw2 = k_ref[k_rows, :]
        vw2 = v_ref[k_rows, :]
        outs = []
        for hh, mask in enumerate((left, ~left)):
            bias_rows = []
            for rq in range(qr):
                r = r0 + rq
                ws = jnp.clip(r - kh // 2, 0, rows - kh)
                tiles = []
                for kp in range(span // 2):
                    ke = s0 + 2 * kp
                    ko = ke + 1
                    ie = jnp.where((ke >= ws) & (ke < ws + kh), ke - r + (WIN_ROWS - 1), masked)
                    io = jnp.where((ko >= ws) & (ko < ws + kh), ko - r + (WIN_ROWS - 1), masked)
                    tiles.append(tab_ref[hh, ie] + tab_ref[hh, 2 * WIN_ROWS + io])
                bias_rows.append(jnp.concatenate(tiles, axis=1))
            bias = jnp.concatenate(bias_rows, axis=0)
            qh = jnp.where(mask, q2, jnp.zeros_like(q2))
            s_loc = _dot_nt(qh, kw2) * c + bias
            s_ctx = _dot_nt(qh, kc2) * c
            m = jnp.maximum(jnp.max(s_loc, axis=-1, keepdims=True), jnp.max(s_ctx, axis=-1, keepdims=True))
            e_loc = jnp.exp2(s_loc - m)
            e_ctx = jnp.exp2(s_ctx - m)
            denom = jnp.sum(e_loc, axis=-1, keepdims=True) + jnp.sum(e_ctx, axis=-1, keepdims=True)
            outs.append((_dot(e_loc.astype(BF16), vw2) + _dot(e_ctx.astype(BF16), vc2)) / denom)
        o_ref[q_rows, :] = jnp.where(left, outs[0], outs[1]).astype(o_ref.dtype)

    def two_blocks(i, carry):
        block(2 * i)
        block(2 * i + 1)
        return carry

    lax.fori_loop(0, rows // qr // 2, two_blocks, 0)


def _nbr_attn(qkv, cache_k, cache_v, tab, layer_j, *, n_batch, len_latent):
    d3 = qkv.shape[1]
    t = n_batch * len_latent
    d = d3 // 3
    hd = d // N_HEADS
    rows = len_latent // GRID_W
    kh = min(WIN_ROWS, rows)
    assert rows % (2 * NBR_QUERY_ROWS) == 0 and rows >= NBR_QUERY_ROWS + kh and (NBR_QUERY_ROWS + kh) % 2 == 0
    assert 2 * hd == LANES
    n_hp = N_HEADS // 2
    pc = cache_k.shape[3]
    qmap = lambda p: (lambda b, h: (b, p * n_hp + h))
    cmap = lambda b, h: (b, layer_j, h, 0, 0)
    return pl.pallas_call(
        functools.partial(_nbr_attn_body, rows=rows, kh=kh, scale=hd ** -0.5),
        grid=(n_batch, n_hp),
        in_specs=[pl.BlockSpec((len_latent, 2 * hd), qmap(0)),
                  pl.BlockSpec((len_latent, 2 * hd), qmap(1)),
                  pl.BlockSpec((len_latent, 2 * hd), qmap(2)),
                  pl.BlockSpec((1, 1, 2, pc, hd), cmap),
                  pl.BlockSpec((1, 1, 2, pc, hd), cmap),
                  pl.BlockSpec((2,) + tab.shape[1:], lambda b, h: (h, 0, 0, 0))],
        out_specs=pl.BlockSpec((len_latent, 2 * hd), lambda b, h: (b, h)),
        out_shape=jax.ShapeDtypeStruct((t, d), BF16),
        compiler_params=_cparams(("arbitrary", "arbitrary")),
        name="nbr_attention",
    )(qkv, qkv, qkv, cache_k, cache_v, tab)


def kernel(x_prompt, x_sample, cache_k, cache_v, c, c_ctx, w_ada, b_ada, ln_g, ln_b, hy_w_in, hy_b_in, hy_conv_w, hy_conv_b, hy_f_w1, hy_f_b1, hy_f_w2, hy_f_b2, hy_f_w3, hy_f_freq, hy_decay, hy_d, hy_w_out, hy_b_out, na_w_qkv, na_b_qkv, na_rpb, na_w_out, na_b_out, moe_w_group, moe_b_group, moe_w_expert, moe_b_expert, moe_w_up, moe_w_down):
    n_ctx_batch, len_ctx, d = x_prompt.shape
    n_lat_batch, len_latent, _ = x_sample.shape
    depth = w_ada.shape[0]
    alpha = (2 * depth) ** 0.25
    n_latent_rows = n_lat_batch * len_latent
    n_ctx_rows = n_ctx_batch * len_ctx
    assert len_ctx == CONV_BLOCK_CTX and len_latent % CONV_BLOCK_LATENT == 0
    nb_latent = len_latent // CONV_BLOCK_LATENT

    x_parts = [x_sample.reshape(n_latent_rows, d), x_prompt.reshape(n_ctx_rows, d)]

    n_cond = 16
    cond = jnp.concatenate([c, c_ctx[None, :], jnp.zeros((n_cond - n_lat_batch - 1, d), F32)], axis=0)
    mod_raw = _adaln(cond, w_ada, b_ada)
    table = mod_raw.reshape(depth, n_cond, 6, d).transpose(0, 2, 1, 3).reshape(depth * 6 * n_cond, 1, d)
    mod = _Mod(table, n_cond, len_latent, n_lat_batch)

    dft_lat = [jnp.asarray(m) for m in _dft_matrices(CONV_BLOCK_LATENT)]
    dft_ctx = [jnp.asarray(m) for m in _dft_matrices(CONV_BLOCK_CTX)]

    new_k_layers, new_v_layers = [], []
    for i in range(depth):
        j = i // 2
        if i % 2 == 0:
            if len(x_parts) == 1:
                x_parts = [x_parts[0][:n_latent_rows], x_parts[0][n_latent_rows:]]
            tm = len_latent
            w_bf = hy_w_in[j].astype(BF16)
            conv = (hy_b_in[j], hy_conv_w[j], hy_conv_b[j])
            x0_l, u_l = _hyena_in(x_parts[0], mod, i, w_bf, *conv, seq_len=len_latent, row0=0, tm=tm)
            x0_c, u_c = _hyena_in(x_parts[1], mod, i, w_bf, *conv, seq_len=len_ctx, row0=n_latent_rows, tm=tm)
            filt = (hy_f_w1[j], hy_f_b1[j], hy_f_w2[j], hy_f_b2[j], hy_f_w3[j], hy_f_freq[j], hy_decay[j])
            spec_lat = _filter_spectra(nb_latent, len_latent, CONV_BLOCK_LATENT, *filt, dft_lat[0])
            spec_ctx = _filter_spectra(1, len_ctx, CONV_BLOCK_CTX, *filt, dft_ctx[0])
            a_parts = [
                _hyena_conv(u_l, x0_l, spec_lat, hy_d[j], dft_lat[0], dft_lat[1], block=CONV_BLOCK_LATENT,
                            n_seq=1, nb=nb_latent),
                _hyena_conv(u_c, x0_c, spec_ctx, hy_d[j], dft_ctx[0], dft_ctx[1], block=CONV_BLOCK_CTX,
                            n_seq=tm // len_ctx, nb=1),
            ]
            w_o, b_o = hy_w_out[j], hy_b_out[j]
        else:
            if len(x_parts) == 2:
                x_parts = [jnp.concatenate(x_parts, axis=0)]
            qkv, nk, nv = _qkv(x_parts[0], mod, i, na_w_qkv[j], na_b_qkv[j], n_latent_rows=n_latent_rows,
                               n_ctx_batch=n_ctx_batch, len_ctx=len_ctx)
            new_k_layers.append(nk)
            new_v_layers.append(nv)
            tab = _nbr_bias_tiles(na_rpb[j])
            a_parts = [
                _nbr_attn(qkv, cache_k, cache_v, tab, j, n_batch=n_lat_batch, len_latent=len_latent),
                _ctx_attn(qkv, row_block0=n_latent_rows // len_ctx, n_batch=n_ctx_batch, len_ctx=len_ctx),
            ]
            w_o, b_o = na_w_out[j], na_b_out[j]
        x, tb, route, counts = _out_proj_route(a_parts, x_parts, w_o, b_o, mod, i, ln_g[i, 0], ln_b[i, 0],
                                               alpha, moe_w_group[i], moe_b_group[i], moe_w_expert[i],
                                               moe_b_expert[i])
        split = n_latent_rows if i == depth - 1 else None
        x = _moe_layer(x, tb, route, counts, mod, i, moe_w_up, moe_w_down, ln_g[i, 1], ln_b[i, 1], alpha,
                       split_rows=split)
        x_parts = list(x) if split is not None else [x]

    y_sample = x_parts[0].reshape(n_lat_batch, len_latent, d)
    y_prompt = x_parts[1].reshape(n_ctx_batch, len_ctx, d)
    new_k = jnp.concatenate(new_k_layers, axis=1)
    new_v = jnp.concatenate(new_v_layers, axis=1)
    return (y_prompt, y_sample, new_k, new_v)
```

```python
import functools
import math

import numpy as np
import jax
import jax.numpy as jnp
from jax import lax
from jax.experimental import pallas as pl
from jax.experimental.pallas import tpu as pltpu

F32 = jnp.float32
BF16 = jnp.bfloat16
HIGHEST = lax.Precision.HIGHEST

GRID_W = 64
N_BANDS = 16
N_HEADS = 16
WIN_ROWS = 8
WIN_COLS = 16
N_GROUPS = 4
EXPERTS_PER_GROUP = 4
LN_EPS = 1e-5
LOG2_E = 1.4426950408889634

LANES = 128
SUBLANES = 8
BF16_ROWS = 16
VMEM_LIMIT_BYTES = 56 * 1024 * 1024

FILTER_PAD = 128
CONV_BLOCK_CTX = 256
CONV_BLOCK_LATENT = 512

MOE_TOKEN_TILE = 512
MOE_ROW_TILE = 512
NBR_QUERY_ROWS = 4


def _cparams(sem):
    return pltpu.CompilerParams(dimension_semantics=sem, vmem_limit_bytes=VMEM_LIMIT_BYTES)


def _dot(a, b, precision=None):
    return jnp.dot(a, b, preferred_element_type=F32, precision=precision)


def _dot_nt(a, b, precision=None):
    return lax.dot_general(a, b, (((1,), (1,)), ((), ())), preferred_element_type=F32,
                           precision=precision)


def _dot3(a, b):
    a_hi = a.astype(BF16)
    a_lo = (a - a_hi.astype(F32)).astype(BF16)
    b_hi = b.astype(BF16)
    b_lo = (b - b_hi.astype(F32)).astype(BF16)
    return _dot(a_hi, b_hi) + (_dot(a_hi, b_lo) + _dot(a_lo, b_hi))


def _silu(x):
    return x / (1.0 + jnp.exp(-x))


def _layer_norm(r, g, b):
    mu = jnp.mean(r, axis=-1, keepdims=True)
    d = r - mu
    var = jnp.mean(d * d, axis=-1, keepdims=True)
    return d * lax.rsqrt(var + LN_EPS) * g + b


def _round_up(x, m):
    return (x + m - 1) // m * m


def _adaln_body(c_ref, w_ref, b_ref, o_ref):
    o_ref[0] = _dot(_silu(c_ref[...]), w_ref[0], HIGHEST) + b_ref[0]


def _adaln(cond, w_ada, b_ada):
    depth, d, d6 = w_ada.shape
    n_cond = cond.shape[0]
    tn = 1024
    return pl.pallas_call(
        _adaln_body,
        grid=(depth, d6 // tn),
        in_specs=[
            pl.BlockSpec((n_cond, d), lambda l, j: (0, 0)),
            pl.BlockSpec((1, d, tn), lambda l, j: (l, 0, j)),
            pl.BlockSpec((1, 1, tn), lambda l, j: (l, 0, j)),
        ],
        out_specs=pl.BlockSpec((1, n_cond, tn), lambda l, j: (l, 0, j)),
        out_shape=jax.ShapeDtypeStruct((depth, n_cond, d6), F32),
        compiler_params=_cparams(("arbitrary", "arbitrary")),
        name="adaln",
    )(cond, w_ada, b_ada.reshape(depth, 1, d6))


class _Mod:
    def __init__(self, table, n_cond, rows_per_cond, n_latent_cond):
        self.table = table
        self.n_cond = n_cond
        self.rows_per_cond = rows_per_cond
        self.n_latent_cond = n_latent_cond

    def spec(self, layer, which, tm, row0=0):
        base = (layer * 6 + which) * self.n_cond
        rpc, nl = self.rows_per_cond, self.n_latent_cond
        d = self.table.shape[-1]

        def index_map(i, *_):
            return (base + jnp.minimum((row0 + i * tm) // rpc, nl), 0, 0)

        return pl.BlockSpec((1, 1, d), index_map)


def _hyena_in_body(x_ref, sh_ref, sc_ref, w0_ref, w1_ref, w2_ref, b0_ref, b1_ref, b2_ref,
                   cw0_ref, cw1_ref, cw2_ref, cb0_ref, cb1_ref, cb2_ref,
                   x0_ref, u_ref, h_ref, *, seq_len):
    j = pl.program_id(1)
    tm = x_ref.shape[0]

    @pl.when(j == 0)
    def _():
        h_ref[...] = (x_ref[...] * (1.0 + sc_ref[0]) + sh_ref[0]).astype(BF16)

    pos =lax.broadcasted_iota(jnp.int32, (tm, 1), 0) & (seq_len - 1)
    first = pos == 0
    last = pos == seq_len - 1

    def part(w_ref, b_ref, cw_ref, cb_ref):
        z = _dot(h_ref[...], w_ref[...]) + b_ref[...]
        z_prev = jnp.where(first, 0.0, pltpu.roll(z, 1, 0))
        z_next = jnp.where(last, 0.0, pltpu.roll(z, tm - 1, 0))
        cw = cw_ref[...]
        return cb_ref[...] + z_prev * cw[0:1] + z * cw[1:2] + z_next * cw[2:3]

    x0_ref[...] = part(w0_ref, b0_ref, cw0_ref, cb0_ref).astype(x0_ref.dtype)
    x1 = part(w1_ref, b1_ref, cw1_ref, cb1_ref)
    v = part(w2_ref, b2_ref, cw2_ref, cb2_ref)
    u_ref[...] = (v * x1).astype(u_ref.dtype)


def _hyena_in(x, mod, layer, w_bf, b_in, conv_w, conv_b, *, seq_len, row0, tm):
    t, d = x.shape
    tn = 256
    assert t % tm == 0 and tm % seq_len == 0 and seq_len & (seq_len - 1) == 0
    nj = d // tn
    b2d = b_in.reshape(1, 3 * d)
    cb2d = conv_b.reshape(1, 3 * d)

    def col(p):
        return lambda i, j: (0, p * nj + j)

    out = jax.ShapeDtypeStruct((t, d), BF16)
    return pl.pallas_call(
        functools.partial(_hyena_in_body, seq_len=seq_len),
        grid=(t // tm, nj),
        in_specs=[pl.BlockSpec((tm, d), lambda i, j: (i, 0)),
                  mod.spec(layer, 0, tm, row0), mod.spec(layer, 1, tm, row0)]
        + [pl.BlockSpec((d, tn), col(p)) for p in range(3)]
        + [pl.BlockSpec((1, tn), col(p)) for p in range(3)]
        + [pl.BlockSpec((3, tn), col(p)) for p in range(3)]
        + [pl.BlockSpec((1, tn), col(p)) for p in range(3)],
        out_specs=[pl.BlockSpec((tm, tn), lambda i, j: (i, j))] * 2,
        out_shape=[out, out],
        scratch_shapes=[pltpu.VMEM((tm, d), BF16)],
        compiler_params=_cparams(("arbitrary", "arbitrary")),
        name=f"hyena_in_{seq_len}",
    )(x, mod.table, mod.table, w_bf, w_bf, w_bf, b2d, b2d, b2d,
      conv_w, conv_w, conv_w, cb2d, cb2d, cb2d)


def _freq_pad(block):
    return _round_up(block + 1, SUBLANES)


def _dft_matrices(block):
    n = 2 * block
    nf = block + 1
    fp = _freq_pad(block)
    f = np.arange(nf, dtype=np.float64)[:, None]
    m = np.arange(n, dtype=np.float64)[None, :]
    ang = 2.0 * np.pi * f * m / n
    fwd = np.zeros((2 * fp, n), np.float64)
    fwd[:nf] = np.cos(ang)
    fwd[fp:fp + nf] = -np.sin(ang)
    a = np.arange(block, dtype=np.float64)[:, None]
    fr = np.arange(nf, dtype=np.float64)[None, :]
    ang_i = 2.0 * np.pi * a * fr / n
    weight = np.full((1, nf), 2.0)
    weight[0, 0] = 1.0
    weight[0, nf - 1] = 1.0
    inv = np.zeros((block, 2 * fp), np.float64)
    inv[:, :nf] = weight * np.cos(ang_i) / n
    inv[:, fp:fp + nf] = -weight * np.sin(ang_i) / n
    return fwd.astype(np.float32), inv.astype(np.float32)


def _lag_tables(nb, seq_len, block):
    n = 2 * block
    lag = np.zeros((2 * nb - 1, n, 1), np.float32)
    valid = np.zeros((2 * nb - 1, n, 1), np.float32)
    m = np.arange(n)
    for dd in range(2 * nb - 1):
        delta = dd - (nb - 1)
        lg = np.where(m < block, block * delta + m, block * delta + m - n)
        ok = (m != block) & (np.abs(lg) <= seq_len - 1)
        lag[dd, :, 0] = np.where(ok, lg, 0)
        valid[dd, :, 0] = ok
    return lag, valid


def _filter_body(lag_ref, valid_ref, bands_ref, w1t_ref, w1c_ref, w1s_ref, b1_ref, w2_ref, b2_ref,
                 freq_ref, w3_ref, dec_ref, dft_ref, k_ref, *, seq_len):
    d = k_ref.shape[-1]
    lag = lag_ref[0]
    pos = jnp.abs(lag)
    tt = pos / float(max(seq_len - 1, 1))
    ang = (2.0 * math.pi / seq_len) * pos * bands_ref[...]
    freq = freq_ref[...]
    pre = (tt * w1t_ref[...] + _dot(jnp.cos(ang), w1c_ref[...], HIGHEST)
           + _dot(-jnp.sin(ang), w1s_ref[...], HIGHEST) + b1_ref[...])
    h = jnp.sin(freq * pre)
    h = jnp.sin(freq * (_dot(h, w2_ref[...], HIGHEST) + b2_ref[...]))
    block = lag.shape[0] // 2
    delta = pl.program_id(0) - (pl.num_programs(0) - 1) // 2

    def half(rows, forward):
        w3 = jnp.where(forward, w3_ref[:, :d], w3_ref[:, d:])
        dec = jnp.where(forward, dec_ref[:, :d], dec_ref[:, d:])
        return _dot3(h[rows], w3) * jnp.exp(-tt[rows] * jnp.abs(dec))

    taps = jnp.concatenate([half(slice(0, block), delta >= 0), half(slice(block, 2 * block), delta >= 1)],
                           axis=0)
    taps = jnp.where(valid_ref[0] > 0.5, taps, 0.0)
    k_ref[0] = _dot3(dft_ref[...], taps)


def _filter_spectra(nb, seq_len, block, f_w1, f_b1, f_w2, f_b2, f_w3, f_freq, decay, dft_fwd):
    hid = f_w1.shape[1]
    d2 = f_w3.shape[1]
    d = d2 // 2
    n = 2 * block
    fp2 = dft_fwd.shape[0]
    lag, valid = _lag_tables(nb, seq_len, block)
    ph = FILTER_PAD - hid
    bands = np.zeros((1, LANES), np.float32)
    bands[0, :N_BANDS] = np.linspace(1e-4, N_BANDS - 1, N_BANDS, dtype=np.float32)
    w1t = jnp.pad(f_w1[0:1], ((0, 0), (0, ph)))
    w1c = jnp.pad(f_w1[1:1 + N_BANDS], ((0, LANES - N_BANDS), (0, ph)))
    w1s = jnp.pad(f_w1[1 + N_BANDS:1 + 2 * N_BANDS], ((0, LANES - N_BANDS), (0, ph)))
    b1 = jnp.pad(f_b1.reshape(1, hid), ((0, 0), (0, ph)))
    w2 = jnp.pad(f_w2, ((0, ph), (0, ph)))
    b2 = jnp.pad(f_b2.reshape(1, hid), ((0, 0), (0, ph)))
    freq = jnp.pad(f_freq.reshape(1, hid), ((0, 0), (0, ph)))
    w3 = jnp.pad(f_w3, ((0, ph), (0, 0)))
    dec = decay.reshape(1, d2)
    nd = 2 * nb - 1

    def whole(shape):
        return pl.BlockSpec(shape, lambda dd: (0,) * len(shape))

    return pl.pallas_call(
        functools.partial(_filter_body, seq_len=seq_len),
        grid=(nd,),
        in_specs=[
            pl.BlockSpec((1, n, 1), lambda dd: (dd, 0, 0)),
            pl.BlockSpec((1, n, 1), lambda dd: (dd, 0, 0)),
            whole((1, LANES)), whole((1, FILTER_PAD)), whole((LANES, FILTER_PAD)),
            whole((LANES, FILTER_PAD)), whole((1, FILTER_PAD)), whole((FILTER_PAD, FILTER_PAD)),
            whole((1, FILTER_PAD)), whole((1, FILTER_PAD)),
            whole((FILTER_PAD, d2)), whole((1, d2)), whole((fp2, n)),
        ],
        out_specs=pl.BlockSpec((1, fp2, d), lambda dd: (dd, 0, 0)),
        out_shape=jax.ShapeDtypeStruct((nd, fp2, d), F32),
        compiler_params=_cparams(("arbitrary",)),
        name=f"hyena_filter_{seq_len}",
    )(jnp.asarray(lag), jnp.asarray(valid), jnp.asarray(bands), w1t, w1c, w1s, b1, w2, b2, freq,
      w3, dec, dft_fwd)


def _hyena_conv_body(u_ref, x0_ref, k_ref, dsk_ref, dftu_ref, idft_ref, o_ref, uf_ref, yf_ref,
                     *, n_seq, nb, block):
    fp = k_ref.shape[1] // 2
    ft_rows = SUBLANES

    def fwd(blk, carry):
        rows = pl.ds(pl.multiple_of(blk * block, block), block)
        uf_ref[blk] = _dot(dftu_ref[...], u_ref[rows, :])
        return carry

    lax.fori_loop(0, n_seq * nb, fwd, 0)

    def freq_tile(ft, carry):
        r = pl.multiple_of(ft * ft_rows, ft_rows)
        re = pl.ds(r, ft_rows)
        im = pl.ds(fp + r, ft_rows)
        k_re = [k_ref[dd, re, :] for dd in range(2 * nb - 1)]
        k_im = [k_ref[dd, im, :] for dd in range(2 * nb - 1)]
        for s in range(n_seq):
            u_re = [uf_ref[s * nb + j, re, :] for j in range(nb)]
            u_im = [uf_ref[s * nb + j, im, :] for j in range(nb)]
            for i in range(nb):
                acc_re = None
                acc_im = None
                for j in range(nb):
                    dd = i - j + (nb - 1)
                    t_re = k_re[dd] * u_re[j] - k_im[dd] * u_im[j]
                    t_im = k_re[dd] * u_im[j] + k_im[dd] * u_re[j]
                    acc_re = t_re if acc_re is None else acc_re + t_re
                    acc_im = t_im if acc_im is None else acc_im + t_im
                yf_ref[s * nb + i, re, :] = acc_re
                yf_ref[s * nb + i, im, :] = acc_im
        return carry

    lax.fori_loop(0, fp // ft_rows, freq_tile, 0)

    def inv(blk, carry):
        rows = pl.ds(pl.multiple_of(blk * block, block), block)
        y = _dot(idft_ref[...], yf_ref[blk].astype(BF16))
        u_blk = u_ref[rows, :].astype(F32)
        o_ref[rows, :] = ((y + u_blk * dsk_ref[...]) * x0_ref[rows, :].astype(F32)).astype(o_ref.dtype)
        return carry

    lax.fori_loop(0, n_seq * nb, inv, 0)


def _hyena_conv(u, x0, spectra, d_skip, dft_fwd, dft_inv, *, block, n_seq, nb):
    t, d = u.shape
    ct = 256
    tm = n_seq * nb * block
    nd = 2 * nb - 1
    fp2 = dft_fwd.shape[0]
    rows = lambda c, b: (b, c)
    return pl.pallas_call(
        functools.partial(_hyena_conv_body, n_seq=n_seq, nb=nb, block=block),
        grid=(d // ct, t // tm),
        in_specs=[
            pl.BlockSpec((tm, ct), rows),
            pl.BlockSpec((tm, ct), rows),
            pl.BlockSpec((nd, fp2, ct), lambda c, b: (0, 0, c)),
            pl.BlockSpec((1, ct), lambda c, b: (0, c)),
            pl.BlockSpec((fp2, block), lambda c, b: (0, 0)),
            pl.BlockSpec((block, fp2), lambda c, b: (0, 0)),
        ],
        out_specs=pl.BlockSpec((tm, ct), rows),
        out_shape=jax.ShapeDtypeStruct((t, d), BF16),
        scratch_shapes=[pltpu.VMEM((n_seq * nb, fp2, ct), F32),
                        pltpu.VMEM((n_seq * nb, fp2, ct), F32)],
        compiler_params=_cparams(("arbitrary", "arbitrary")),
        name=f"hyena_conv_nb{nb}",
    )(u, x0, spectra, d_skip.reshape(1, d), dft_fwd[:, :block].astype(BF16), dft_inv.astype(BF16))


def _out_proj_body(*refs, alpha, split_tile, x_parts):
    a_refs = refs[0:2]
    x_refs = refs[2:2 + x_parts]
    (w_ref, b_ref, gate_ref, g_ref, beta_ref, sh_ref, sc_ref, rw_ref, rb_ref,
     o_ref, t_ref, route_ref, cnt_ref) = refs[2 + x_parts:]
    i = pl.program_id(0)

    def run(a_ref, x_ref):
        o = _dot(a_ref[...], w_ref[...]) + b_ref[...]
        r = alpha * x_ref[...] + gate_ref[0] * o
        o_ref[...] = _layer_norm(r, g_ref[...], beta_ref[...])

    @pl.when(i < split_tile)
    def _():
        run(a_refs[0], x_refs[0])

    @pl.when(i >= split_tile)
    def _():
        run(a_refs[1], x_refs[-1])

    t = o_ref[...] * (1.0 + sc_ref[0]) + sh_ref[0]
    t_ref[...] = t.astype(BF16)
    route, counts = _route_tokens(t, rw_ref[...], rb_ref[...])
    route_ref[...] = route
    cnt_ref[0] = jnp.broadcast_to(counts, cnt_ref.shape[1:])


def _out_proj_route(a_parts, x_parts, w, b, mod, layer, ln_g, ln_b, alpha, w_group, b_group, w_expert,
                    b_expert):
    d = a_parts[0].shape[1]
    tm = MOE_TOKEN_TILE
    split_tile = a_parts[0].shape[0] // tm
    t = a_parts[0].shape[0] + a_parts[1].shape[0]
    pad = LANES - N_GROUPS - w_expert.shape[1]
    rw = jnp.pad(jnp.concatenate([w_group, w_expert], axis=1), ((0, 0), (0, pad)))
    rb = jnp.pad(jnp.concatenate([b_group, b_expert]), (0, pad)).reshape(1, LANES)
    first = lambda i: (jnp.minimum(i, split_tile - 1), 0)
    second = lambda i: (jnp.maximum(i - split_tile, 0), 0)
    row = lambda i: (i, 0)
    const = lambda i: (0, 0)
    part_specs = [pl.BlockSpec((tm, d), first), pl.BlockSpec((tm, d), second)]
    x_specs = part_specs if len(x_parts) == 2 else [pl.BlockSpec((tm, d), row)]
    return pl.pallas_call(
        functools.partial(_out_proj_body, alpha=alpha, split_tile=split_tile, x_parts=len(x_parts)),
        grid=(t // tm,),
        in_specs=part_specs + x_specs
        + [pl.BlockSpec((d, d), const), pl.BlockSpec((1, d), const), mod.spec(layer, 2, tm),
           pl.BlockSpec((1, d), const), pl.BlockSpec((1, d), const),
           mod.spec(layer, 3, tm), mod.spec(layer, 4, tm),
           pl.BlockSpec((d, LANES), const), pl.BlockSpec((1, LANES), const)],
        out_specs=[pl.BlockSpec((tm, d), row), pl.BlockSpec((tm, d), row), pl.BlockSpec((tm, LANES), row),
                   pl.BlockSpec((1, SUBLANES, LANES), lambda i: (i, 0, 0))],
        out_shape=[jax.ShapeDtypeStruct((t, d), F32), jax.ShapeDtypeStruct((t, d), BF16),
                   jax.ShapeDtypeStruct((t, LANES), F32),
                   jax.ShapeDtypeStruct((t // tm, SUBLANES, LANES), F32)],
        compiler_params=_cparams(("arbitrary",)),
        name="out_proj_ln_route",
    )(*a_parts, *x_parts, w.astype(BF16), b.reshape(1, d), mod.table,
      ln_g.reshape(1, d), ln_b.reshape(1, d), mod.table, mod.table, rw, rb)


def _route_tokens(t, w, b):
    tm = t.shape[0]
    lg = _dot3(t, w) + b
    lane = lax.broadcasted_iota(jnp.int32, lg.shape, 1)
    neg = -jnp.inf
    far = jnp.int32(LANES)
    g_mask = lane < N_GROUPS
    gl = jnp.where(g_mask, lg, neg)
    g_max = jnp.max(gl, axis=-1, keepdims=True)
    g_sel = jnp.min(jnp.where(gl == g_max, lane, far), axis=-1, keepdims=True)
    g_w = 1.0 / jnp.sum(jnp.where(g_mask, jnp.exp(lg - g_max), 0.0), axis=-1, keepdims=True)
    e_lo = N_GROUPS + EXPERTS_PER_GROUP * g_sel
    e_mask = (lane >= e_lo) & (lane < e_lo + EXPERTS_PER_GROUP)
    el = jnp.where(e_mask, lg, neg)
    m1 = jnp.max(el, axis=-1, keepdims=True)
    i1 = jnp.min(jnp.where(el == m1, lane, far), axis=-1, keepdims=True)
    el2 = jnp.where(lane == i1, neg, el)
    m2 = jnp.max(el2, axis=-1, keepdims=True)
    i2 = jnp.min(jnp.where(el2 == m2, lane, far), axis=-1, keepdims=True)
    ratio = jnp.exp(m2 - m1)
    w1 = 1.0 / (1.0 + ratio)
    w2 = ratio / (1.0 + ratio)
    e1 = i1 - N_GROUPS
    e2 = i2 - N_GROUPS
    hit = (lane == e1) | (lane == e2)
    counts = jnp.sum(hit.astype(F32), axis=0, keepdims=True)
    run_len = jnp.floor((counts + (BF16_ROWS - 1)) * (1.0 / BF16_ROWS)) * BF16_ROWS
    lower = (lax.broadcasted_iota(jnp.int32, (LANES, LANES), 0)
             < lax.broadcasted_iota(jnp.int32, (LANES, LANES), 1)).astype(F32)
    starts = _dot(jnp.broadcast_to(run_len, (SUBLANES, LANES)), lower, HIGHEST)[0:1]
    earlier = (lax.broadcasted_iota(jnp.int32, (tm, tm), 1)
               < lax.broadcasted_iota(jnp.int32, (tm, tm), 0)).astype(BF16)
    pos = _dot(earlier, hit.astype(BF16)) + starts
    p1 = jnp.sum(jnp.where(lane == e1, pos, 0.0), axis=-1, keepdims=True)
    p2 = jnp.sum(jnp.where(lane == e2, pos, 0.0), axis=-1, keepdims=True)
    fields = (e1.astype(F32), e2.astype(F32), g_w * w1, g_w * w2, p1, p2)
    route = jnp.zeros(lg.shape, F32)
    for k, val in enumerate(fields):
        route = jnp.where(lane == k, val, route)
    return route, counts


def _moe_layout(counts, n_experts, n_row_tiles_max, chunk_slots):
    cnt = counts[:, 0, :n_experts].astype(jnp.int32)
    run_len = (cnt + (BF16_ROWS - 1)) // BF16_ROWS * BF16_ROWS
    total = jnp.sum(run_len, axis=0)
    region = (total + (MOE_ROW_TILE - 1)) // MOE_ROW_TILE * MOE_ROW_TILE
    region_end = jnp.cumsum(region)
    expert_start = region_end - region
    run_start = expert_start[None, :] + jnp.cumsum(run_len, axis=0) - run_len
    n_used = (region_end[-1] // MOE_ROW_TILE).reshape(1).astype(jnp.int32)
    tile_idx = jnp.arange(n_row_tiles_max, dtype=jnp.int32)
    tile_expert = jnp.minimum(
        jnp.sum((tile_idx[:, None] >= (region_end // MOE_ROW_TILE)[None, :]).astype(jnp.int32), axis=1),
        n_experts - 1).astype(jnp.int32)
    used_rows = region_end[-1:]
    tail_chunks = (n_row_tiles_max * MOE_ROW_TILE - used_rows) // (MOE_ROW_TILE // 2)
    local_end = jnp.cumsum(run_len, axis=1)
    pos = jnp.arange(chunk_slots, dtype=jnp.int32) * BF16_ROWS
    run_of_chunk = jnp.sum((pos[None, :, None] >= local_end[:, None, :]).astype(jnp.int32), axis=-1)
    in_run = run_of_chunk[:, :, None] == jnp.arange(n_experts, dtype=jnp.int32)[None, None, :]
    shift = (run_start - (local_end - run_len))[:, None, :]
    chunk_row = jnp.sum(jnp.where(in_run, shift, 0), axis=-1) + pos[None, :]
    chunk_row = jnp.where(pos[None, :] < local_end[:, -1:], chunk_row, 0)
    return dict(chunk_row=chunk_row.reshape(-1).astype(jnp.int32),
                n_chunks=(local_end[:, -1] // BF16_ROWS).astype(jnp.int32),
                gap_start=jnp.concatenate([expert_start + total, used_rows]).astype(jnp.int32),
                gap_len=jnp.concatenate([region - total, tail_chunks]).astype(jnp.int32),
                tile_expert=tile_expert, n_used=n_used)


def _run_chunk_loops(chunk_copy, n_chunks):
    def start(c, carry):
        chunk_copy(c).start()
        return carry

    def wait(c, carry):
        chunk_copy(c).wait()
        return carry

    return (lambda: lax.fori_loop(0, n_chunks, start, 0)), (lambda: lax.fori_loop(0, n_chunks, wait, 0))


def _chunk_copies(src_ref, src_off, dst_ref, dst_off, n, sem, max_chunk, advance_src=True):
    out = []
    off = jnp.int32(0)
    bit = max_chunk
    while bit >= BF16_ROWS:
        take = n & bit
        s = pl.multiple_of(src_off + off, BF16_ROWS) if advance_src else src_off
        t = pl.multiple_of(dst_off + off, BF16_ROWS)
        cp = pltpu.make_async_copy(src_ref.at[pl.ds(s, bit)], dst_ref.at[pl.ds(t, bit)], sem)
        out.append((take != 0, cp))
        off = off + take
        bit //= 2
    return out


def _start_all(copies):
    for cond, cp in copies:
        pl.when(cond)(cp.start)


def _wait_all(copies):
    for cond, cp in copies:
        pl.when(cond)(cp.wait)


def _dispatch_body(cd_ref, nch_ref, gs_ref, gl_ref, t_ref, route_ref, xg_ref, sorted_ref, zero_ref, sems,
                   *, n_experts):
    tile = pl.program_id(0)
    last_tile = pl.num_programs(0) - 1
    buf = lax.rem(tile, 2)
    sem = sems.at[0]
    tm = t_ref.shape[0]
    n_rows = sorted_ref.shape[1]
    slots = n_rows // BF16_ROWS
    route = route_ref[...]
    lane = lax.broadcasted_iota(jnp.int32, (tm, LANES), 1)
    both = jnp.where(lane == 0, route[:, 4:5], jnp.where(lane == 1, route[:, 5:6], 0.0))
    pick = (lax.broadcasted_iota(jnp.int32, (SUBLANES, LANES), 0)
            == lax.broadcasted_iota(jnp.int32, (SUBLANES, LANES), 1)).astype(F32)
    as_rows = _dot_nt(pick, both, HIGHEST)
    r_iota = lax.broadcasted_iota(jnp.int32, (n_rows, tm), 0).astype(F32)
    select = ((r_iota == as_rows[0:1]) | (r_iota == as_rows[1:2])).astype(BF16)
    sorted_ref[buf] = _dot(select, t_ref[...]).astype(BF16)

    def chunk_loops(tl, b):
        def chunk_copy(c):
            src = pl.multiple_of(c * BF16_ROWS, BF16_ROWS)
            dst = pl.multiple_of(cd_ref[tl * slots + c], BF16_ROWS)
            return pltpu.make_async_copy(sorted_ref.at[b, pl.ds(src, BF16_ROWS)],
                                         xg_ref.at[pl.ds(dst, BF16_ROWS)], sems.at[b])

        return _run_chunk_loops(chunk_copy, nch_ref[tl])

    start_chunks, wait_chunks = chunk_loops(tile, buf)
    start_chunks()

    @pl.when(tile > 0)
    def _():
        chunk_loops(tile - 1, 1 - buf)[1]()

    @pl.when(tile == last_tile)
    def _():
        wait_chunks()
        zero_ref[...] = jnp.zeros_like(zero_ref)
        fills = []
        for e in range(n_experts):
            fills += _chunk_copies(zero_ref, 0, xg_ref, gs_ref[e], gl_ref[e], sem, zero_ref.shape[0],
                                   advance_src=False)
        _start_all(fills)
        _wait_all(fills)
        z_rows = zero_ref.shape[0]
        tail0 = gs_ref[n_experts]

        def tail_copy(c):
            dst = pl.multiple_of(tail0 + c * z_rows, z_rows)
            return pltpu.make_async_copy(zero_ref, xg_ref.at[pl.ds(dst, z_rows)], sem)

        def start(c, carry):
            tail_copy(c).start()
            return carry

        def wait(c, carry):
            tail_copy(c).wait()
            return carry

        lax.fori_loop(0, gl_ref[n_experts], start, 0)
        lax.fori_loop(0, gl_ref[n_experts], wait, 0)


def _sorted_rows(tm, n_experts):
    return _round_up(2 * tm + n_experts * (BF16_ROWS - 1), LANES)


def _dispatch(tb, route, layout, n_experts, n_rows_max):
    t, d = tb.shape
    tm = MOE_TOKEN_TILE
    row = lambda i, *_: (i, 0)
    grid_spec = pltpu.PrefetchScalarGridSpec(
        num_scalar_prefetch=4,
        grid=(t // tm,),
        in_specs=[pl.BlockSpec((tm, d), row), pl.BlockSpec((tm, LANES), row)],
        out_specs=pl.BlockSpec(memory_space=pl.ANY),
        scratch_shapes=[pltpu.VMEM((2, _sorted_rows(tm, n_experts), d), BF16),
                        pltpu.VMEM((MOE_ROW_TILE // 2, d), BF16),
                        pltpu.SemaphoreType.DMA((2,))],
    )
    return pl.pallas_call(
        functools.partial(_dispatch_body, n_experts=n_experts),
        grid_spec=grid_spec,
        out_shape=jax.ShapeDtypeStruct((n_rows_max, d), BF16),
        compiler_params=_cparams(("arbitrary",)),
        name="moe_dispatch",
    )(layout["chunk_row"], layout["n_chunks"], layout["gap_start"], layout["gap_len"], tb, route)


def _experts_body(te_ref, nu_ref, x_ref, wu_ref, wd_ref, y_ref, wub_ref, wdb_ref, *, d_expert):
    g = pl.program_id(0)

    @pl.when(g < nu_ref[0])
    def _():
        prev = te_ref[jnp.maximum(g - 1, 0)]

        @pl.when((g == 0) | (te_ref[g] != prev))
        def _():
            wub_ref[...] = wu_ref[0, 0].astype(BF16)
            wdb_ref[...] = wd_ref[0, 0].astype(BF16)

        ab = _dot(x_ref[...], wub_ref[...])
        h = _silu(ab[:, :d_expert]) * ab[:, d_expert:]
        y_ref[...] = _dot(h.astype(BF16), wdb_ref[...]).astype(y_ref.dtype)

    @pl.when(g >= nu_ref[0])
    def _():
        y_ref[...] = jnp.zeros_like(y_ref)


def _experts(xg, w_up, w_down, layer, layout):
    n_rows, d = xg.shape
    d_up = w_up.shape[-1]
    d_expert = d_up // 2
    tm = MOE_ROW_TILE
    used = lambda g, nu: jnp.maximum(jnp.minimum(g, nu[0] - 1), 0)
    grid_spec = pltpu.PrefetchScalarGridSpec(
        num_scalar_prefetch=2,
        grid=(n_rows // tm,),
        in_specs=[pl.BlockSpec((tm, d), lambda g, te, nu: (used(g, nu), 0)),
                  pl.BlockSpec((1, 1, d, d_up), lambda g, te, nu: (layer, te[used(g, nu)], 0, 0)),
                  pl.BlockSpec((1, 1, d_expert, d), lambda g, te, nu: (layer, te[used(g, nu)], 0, 0))],
        out_specs=pl.BlockSpec((tm, d), lambda g, te, nu: (g, 0)),
        scratch_shapes=[pltpu.VMEM((d, d_up), BF16), pltpu.VMEM((d_expert, d), BF16)],
    )
    return pl.pallas_call(
        functools.partial(_experts_body, d_expert=d_expert),
        grid_spec=grid_spec,
        out_shape=jax.ShapeDtypeStruct((n_rows, d), BF16),
        compiler_params=_cparams(("arbitrary",)),
        name="moe_experts",
    )(layout["tile_expert"], layout["n_used"], xg, w_up, w_down)


def _combine_body(cd_ref, nch_ref, yg_ref, route_ref, x_ref, gate_ref, g_ref, beta_ref, *rest,
                  alpha, split_tile):
    if split_tile is None:
        o_ref, ybuf_ref, sems = rest
    else:
        o_ref, o2_ref, ybuf_ref, sems = rest
    tile = pl.program_id(0)
    buf = lax.rem(tile, 2)
    tm = x_ref.shape[0]
    n_rows = ybuf_ref.shape[1]
    slots = n_rows // BF16_ROWS

    def chunk_loops(tl, b):
        def chunk_copy(c):
            src = pl.multiple_of(cd_ref[tl * slots + c], BF16_ROWS)
            dst = pl.multiple_of(c * BF16_ROWS, BF16_ROWS)
            return pltpu.make_async_copy(yg_ref.at[pl.ds(src, BF16_ROWS)],
                                         ybuf_ref.at[b, pl.ds(dst, BF16_ROWS)], sems.at[b])

        return _run_chunk_loops(chunk_copy, nch_ref[tl])

    def fetch(tl, b):
        ybuf_ref[b, pl.ds(2 * tm, n_rows - 2 * tm), :] = jnp.zeros(
            (n_rows - 2 * tm, ybuf_ref.shape[2]), ybuf_ref.dtype)
        chunk_loops(tl, b)[0]()

    @pl.when(tile == 0)
    def _():
        fetch(tile, buf)

    @pl.when(tile + 1 < pl.num_programs(0))
    def _():
        fetch(tile + 1, 1 - buf)

    route = route_ref[...]
    r_lane = lax.broadcasted_iota(jnp.int32, (tm, n_rows), 1).astype(F32)
    cmat = (jnp.where(r_lane == route[:, 4:5], route[:, 2:3], 0.0)
            + jnp.where(r_lane == route[:, 5:6], route[:, 3:4], 0.0)).astype(BF16)
    chunk_loops(tile, buf)[1]()
    y = _dot(cmat, ybuf_ref[buf])
    r = alpha * x_ref[...] + gate_ref[0] * y
    res = _layer_norm(r, g_ref[...], beta_ref[...])
    if split_tile is None:
        o_ref[...] = res
    else:
        @pl.when(tile < split_tile)
        def _():
            o_ref[...] = res

        @pl.when(tile >= split_tile)
        def _():
            o2_ref[...] = res


def _combine(yg, route, x, layout, mod, layer, ln_g, ln_b, alpha, n_experts, split_rows=None):
    t, d = x.shape
    tm = MOE_TOKEN_TILE
    row = lambda i, *_: (i, 0)
    const = lambda i, *_: (0, 0)
    if split_rows is None:
        split_tile = None
        out_specs = pl.BlockSpec((tm, d), row)
        out_shape = jax.ShapeDtypeStruct((t, d), F32)
    else:
        assert split_rows % tm == 0
        split_tile = split_rows // tm
        out_specs = [pl.BlockSpec((tm, d), lambda i, *_: (jnp.minimum(i, split_tile - 1), 0)),
                     pl.BlockSpec((tm, d), lambda i, *_: (jnp.maximum(i - split_tile, 0), 0))]
        out_shape = [jax.ShapeDtypeStruct((split_rows, d), F32),
                     jax.ShapeDtypeStruct((t - split_rows, d), F32)]
    grid_spec = pltpu.PrefetchScalarGridSpec(
        num_scalar_prefetch=2,
        grid=(t // tm,),
        in_specs=[pl.BlockSpec(memory_space=pl.ANY), pl.BlockSpec((tm, LANES), row),
                  pl.BlockSpec((tm, d), row), mod.spec(layer, 5, tm),
                  pl.BlockSpec((1, d), const), pl.BlockSpec((1, d), const)],
        out_specs=out_specs,
        scratch_shapes=[pltpu.VMEM((2, _sorted_rows(tm, n_experts), d), BF16),
                        pltpu.SemaphoreType.DMA((2,))],
    )
    return pl.pallas_call(
        functools.partial(_combine_body, alpha=alpha, split_tile=split_tile),
        grid_spec=grid_spec,
        out_shape=out_shape,
        compiler_params=_cparams(("arbitrary",)),
        name="moe_combine_ln",
    )(layout["chunk_row"], layout["n_chunks"], yg, route, x, mod.table,
      ln_g.reshape(1, d), ln_b.reshape(1, d))


def _moe_layer(x, tb, route, counts, mod, layer, w_up, w_down, ln_g, ln_b, alpha, split_rows=None):
    t, d = x.shape
    n_e = w_up.shape[1]
    n_tok_tiles = t // MOE_TOKEN_TILE
    max_rows = 2 * t + n_tok_tiles * n_e * (BF16_ROWS - 1) + n_e * (MOE_ROW_TILE - BF16_ROWS)
    n_rows_max = _round_up(max_rows, MOE_ROW_TILE)
    layout = _moe_layout(counts, n_e, n_rows_max // MOE_ROW_TILE,
                         _sorted_rows(MOE_TOKEN_TILE, n_e) // BF16_ROWS)
    xg = _dispatch(tb, route, layout, n_e, n_rows_max)
    yg = _experts(xg, w_up, w_down, layer, layout)
    return _combine(yg, route, x, layout, mod, layer, ln_g, ln_b, alpha, n_e, split_rows)


def _qkv_body(y_ref, sh_ref, sc_ref, w_ref, b_ref, qkv_ref, nk_ref, nv_ref, *, n_latent_tiles, d):
    i = pl.program_id(0)
    h = (y_ref[...] * (1.0 + sc_ref[0]) + sh_ref[0]).astype(BF16)
    z = _dot(h, w_ref[...]) + b_ref[...]
    qkv_ref[...] = z.astype(BF16)

    @pl.when(i >= n_latent_tiles)
    def _():
        hd = nk_ref.shape[-1]
        seq = nk_ref.shape[3]
        for s in range(nk_ref.shape[0]):
            rows = slice(s * seq, (s + 1) * seq)
            for hh in range(nk_ref.shape[2]):
                nk_ref[s, 0, hh] = z[rows, d + hh * hd:d + (hh + 1) * hd]
                nv_ref[s, 0, hh] = z[rows, 2 * d + hh * hd:2 * d + (hh + 1) * hd]


def _qkv(y, mod, layer, w, b, *, n_latent_rows, n_ctx_batch, len_ctx):
    t, d = y.shape
    seq_per_tile = 2
    tm = seq_per_tile * len_ctx
    hd = d // N_HEADS
    assert n_latent_rows % tm == 0 and n_ctx_batch % seq_per_tile == 0
    n_lat = n_latent_rows // tm
    kv_map = lambda i: (jnp.maximum(i - n_lat, 0), 0, 0, 0, 0)
    kv_shape = jax.ShapeDtypeStruct((n_ctx_batch, 1, N_HEADS, len_ctx, hd), F32)
    return pl.pallas_call(
        functools.partial(_qkv_body, n_latent_tiles=n_lat, d=d),
        grid=(t // tm,),
        in_specs=[pl.BlockSpec((tm, d), lambda i: (i, 0)), mod.spec(layer, 0, tm), mod.spec(layer, 1, tm),
                  pl.BlockSpec((d, 3 * d), lambda i: (0, 0)), pl.BlockSpec((1, 3 * d), lambda i: (0, 0))],
        out_specs=[pl.BlockSpec((tm, 3 * d), lambda i: (i, 0)),
                   pl.BlockSpec((seq_per_tile, 1, N_HEADS, len_ctx, hd), kv_map),
                   pl.BlockSpec((seq_per_tile, 1, N_HEADS, len_ctx, hd), kv_map)],
        out_shape=[jax.ShapeDtypeStruct((t, 3 * d), BF16), kv_shape, kv_shape],
        compiler_params=_cparams(("arbitrary",)),
        name="attn_qkv",
    )(y, mod.table, mod.table, w.astype(BF16), b.reshape(1, 3 * d))


def _ctx_attn_body(qkv_ref, o_ref, *, d, scale):
    pair = 2 * (d // N_HEADS)
    left = lax.broadcasted_iota(jnp.int32, (1, pair), 1) < pair // 2
    c = scale * LOG2_E
    for hp in range(d // pair):
        cols = slice(hp * pair, (hp + 1) * pair)
        q2 = qkv_ref[:, cols]
        k2 = qkv_ref[:, d + hp * pair:d + (hp + 1) * pair]
        v2 = qkv_ref[:, 2 * d + hp * pair:2 * d + (hp + 1) * pair]
        outs = []
        for mask in (left, ~left):
            s = _dot_nt(jnp.where(mask, q2, jnp.zeros_like(q2)), k2) * c
            e = jnp.exp2(s - jnp.max(s, axis=-1, keepdims=True))
            p = (e / jnp.sum(e, axis=-1, keepdims=True)).astype(BF16)
            outs.append(_dot(p, v2))
        o_ref[:, cols] = jnp.where(left, outs[0], outs[1]).astype(o_ref.dtype)


def _ctx_attn(qkv, *, row_block0, n_batch, len_ctx):
    d3 = qkv.shape[1]
    d = d3 // 3
    scale = (d // N_HEADS) ** -0.5
    return pl.pallas_call(
        functools.partial(_ctx_attn_body, d=d, scale=scale),
        grid=(n_batch,),
        in_specs=[pl.BlockSpec((len_ctx, d3), lambda b: (row_block0 + b, 0))],
        out_specs=pl.BlockSpec((len_ctx, d), lambda b: (b, 0)),
        out_shape=jax.ShapeDtypeStruct((n_batch * len_ctx, d), BF16),
        compiler_params=_cparams(("arbitrary",)),
        name="ctx_attention",
    )(qkv)


def _nbr_bias_body(rpb_ref, onehot_ref, band_ref, o_ref):
    picked = _dot(rpb_ref[...], onehot_ref[...], HIGHEST)
    o_ref[...] = jnp.where(band_ref[...] > 0.5, picked * LOG2_E, -jnp.inf)


def _nbr_bias_tiles(rpb):
    n_h, n_dr, n_dc = rpb.shape
    col = np.arange(GRID_W)
    col_start = np.clip(col - WIN_COLS // 2, 0, GRID_W - WIN_COLS)
    in_band = (col[None, :] >= col_start[:, None]) & (col[None, :] < col_start[:, None] + WIN_COLS)
    dc_idx = np.clip(col[None, :] - col[:, None], -(WIN_COLS - 1), WIN_COLS - 1) + (WIN_COLS - 1)
    n_pairs = GRID_W * GRID_W
    onehot = np.zeros((LANES, n_pairs), np.float32)
    onehot[dc_idx.reshape(-1), np.arange(n_pairs)] = 1.0
    band = in_band.reshape(1, n_pairs).astype(np.float32)
    rows = n_h * n_dr
    rpb2d = jnp.pad(rpb.reshape(rows, n_dc), ((0, 0), (0, LANES - n_dc)))
    whole = lambda shape: pl.BlockSpec(shape, lambda i: (0,) * len(shape))
    cols = pl.pallas_call(
        _nbr_bias_body,
        grid=(1,),
        in_specs=[whole((rows, LANES)), whole((LANES, n_pairs)), whole((1, n_pairs))],
        out_specs=whole((rows, n_pairs)),
        out_shape=jax.ShapeDtypeStruct((rows, n_pairs), F32),
        compiler_params=_cparams(("arbitrary",)),
        name="nbr_bias",
    )(rpb2d, jnp.asarray(onehot), jnp.asarray(band))
    cols = cols.reshape(n_h, n_dr, GRID_W, GRID_W)
    n_entries = 2 * WIN_ROWS
    masked = jnp.full((n_h, n_entries - n_dr, GRID_W, GRID_W), -jnp.inf, F32)
    cols = jnp.concatenate([cols, masked], axis=1)
    zeros = jnp.zeros_like(cols)
    left = jnp.concatenate([cols, zeros], axis=-1)
    right = jnp.concatenate([zeros, cols], axis=-1)
    return jnp.concatenate([left, right], axis=1)


def _nbr_attn_body(q_ref, k_ref, v_ref, ck_ref, cv_ref, tab_ref, o_ref, *, rows, kh, scale):
    hd = ck_ref.shape[-1]
    qr = NBR_QUERY_ROWS
    span = qr + kh
    n_q = qr * GRID_W
    n_k = span * GRID_W
    masked = 2 * WIN_ROWS - 1
    c = scale * LOG2_E
    left = lax.broadcasted_iota(jnp.int32, (1, 2 * hd), 1) < hd
    kc2 = jnp.concatenate([ck_ref[0, 0, 0], ck_ref[0, 0, 1]], axis=1).astype(BF16)
    vc2 = jnp.concatenate([cv_ref[0, 0, 0], cv_ref[0, 0, 1]], axis=1).astype(BF16)

    def block(qb):
        r0 = qb * qr
        s0 = jnp.clip(r0 - kh // 2, 0, rows - span)
        q_rows = pl.ds(pl.multiple_of(r0 * GRID_W, n_q), n_q)
        k_rows = pl.ds(pl.multiple_of(s0 * GRID_W, GRID_W), n_k)
        q2 = q_ref[q_rows, :]
        kw2 = k_ref[k_rows, :]
        vw2 = v_ref[k_rows, :]
        outs = []
        for hh, mask in enumerate((left, ~left)):
            bias_rows = []
            for rq in range(qr):
                r = r0 + rq
                ws = jnp.clip(r - kh // 2, 0, rows - kh)
                tiles = []
                for kp in range(span // 2):
                    ke = s0 + 2 * kp
                    ko = ke + 1
                    ie = jnp.where((ke >= ws) & (ke < ws + kh), ke - r + (WIN_ROWS - 1), masked)
                    io = jnp.where((ko >= ws) & (ko < ws + kh), ko - r + (WIN_ROWS - 1), masked)
                    tiles.append(tab_ref[hh, ie] + tab_ref[hh, 2 * WIN_ROWS + io])
                bias_rows.append(jnp.concatenate(tiles, axis=1))
            bias = jnp.concatenate(bias_rows, axis=0)
            qh = jnp.where(mask, q2, jnp.zeros_like(q2))
            s_loc = _dot_nt(qh, kw2) * c + bias
            s_ctx = _dot_nt(qh, kc2) * c
            m = jnp.maximum(jnp.max(s_loc, axis=-1, keepdims=True), jnp.max(s_ctx, axis=-1, keepdims=True))
            e_loc = jnp.exp2(s_loc - m)
            e_ctx = jnp.exp2(s_ctx - m)
            denom = jnp.sum(e_loc, axis=-1, keepdims=True) + jnp.sum(e_ctx, axis=-1, keepdims=True)
            outs.append((_dot(e_loc.astype(BF16), vw2) + _dot(e_ctx.astype(BF16), vc2)) / denom)
        o_ref[q_rows, :] = jnp.where(left, outs[0], outs[1]).astype(o_ref.dtype)

    def two_blocks(i, carry):
        block(2 * i)
        block(2 * i + 1)
        return carry

    lax.fori_loop(0, rows // qr // 2, two_blocks, 0)


def _nbr_attn(qkv, cache_k, cache_v, tab, layer_j, *, n_batch, len_latent):
    d3 = qkv.shape[1]
    t = n_batch * len_latent
    d = d3 // 3
    hd = d // N_HEADS
    rows = len_latent // GRID_W
    kh = min(WIN_ROWS, rows)
    assert rows % (2 * NBR_QUERY_ROWS) == 0 and rows >= NBR_QUERY_ROWS + kh and (NBR_QUERY_ROWS + kh) % 2 == 0
    assert 2 * hd == LANES
    n_hp = N_HEADS // 2
    pc = cache_k.shape[3]
    qmap = lambda p: (lambda b, h: (b, p * n_hp + h))
    cmap = lambda b, h: (b, layer_j, h, 0, 0)
    return pl.pallas_call(
        functools.partial(_nbr_attn_body, rows=rows, kh=kh, scale=hd ** -0.5),
        grid=(n_batch, n_hp),
        in_specs=[pl.BlockSpec((len_latent, 2 * hd), qmap(0)),
                  pl.BlockSpec((len_latent, 2 * hd), qmap(1)),
                  pl.BlockSpec((len_latent, 2 * hd), qmap(2)),
                  pl.BlockSpec((1, 1, 2, pc, hd), cmap),
                  pl.BlockSpec((1, 1, 2, pc, hd), cmap),
                  pl.BlockSpec((2,) + tab.shape[1:], lambda b, h: (h, 0, 0, 0))],
        out_specs=pl.BlockSpec((len_latent, 2 * hd), lambda b, h: (b, h)),
        out_shape=jax.ShapeDtypeStruct((t, d), BF16),
        compiler_params=_cparams(("arbitrary", "arbitrary")),
        name="nbr_attention",
    )(qkv, qkv, qkv, cache_k, cache_v, tab)


def kernel(x_prompt, x_sample, cache_k, cache_v, c, c_ctx, w_ada, b_ada, ln_g, ln_b, hy_w_in, hy_b_in, hy_conv_w, hy_conv_b, hy_f_w1, hy_f_b1, hy_f_w2, hy_f_b2, hy_f_w3, hy_f_freq, hy_decay, hy_d, hy_w_out, hy_b_out, na_w_qkv, na_b_qkv, na_rpb, na_w_out, na_b_out, moe_w_group, moe_b_group, moe_w_expert, moe_b_expert, moe_w_up, moe_w_down):
    n_ctx_batch, len_ctx, d = x_prompt.shape
    n_lat_batch, len_latent, _ = x_sample.shape
    depth = w_ada.shape[0]
    alpha = (2 * depth) ** 0.25
    n_latent_rows = n_lat_batch * len_latent
    n_ctx_rows = n_ctx_batch * len_ctx
    assert len_ctx == CONV_BLOCK_CTX and len_latent % CONV_BLOCK_LATENT == 0
    nb_latent = len_latent // CONV_BLOCK_LATENT

    x_parts = [x_sample.reshape(n_latent_rows, d), x_prompt.reshape(n_ctx_rows, d)]

    n_cond = 16
    cond = jnp.concatenate([c, c_ctx[None, :], jnp.zeros((n_cond - n_lat_batch - 1, d), F32)], axis=0)
    mod_raw = _adaln(cond, w_ada, b_ada)
    table = mod_raw.reshape(depth, n_cond, 6, d).transpose(0, 2, 1, 3).reshape(depth * 6 * n_cond, 1, d)
    mod = _Mod(table, n_cond, len_latent, n_lat_batch)

    dft_lat = [jnp.asarray(m) for m in _dft_matrices(CONV_BLOCK_LATENT)]
    dft_ctx = [jnp.asarray(m) for m in _dft_matrices(CONV_BLOCK_CTX)]

    new_k_layers, new_v_layers = [], []
    for i in range(depth):
        j = i // 2
        if i % 2 == 0:
            if len(x_parts) == 1:
                x_parts = [x_parts[0][:n_latent_rows], x_parts[0][n_latent_rows:]]
            tm = len_latent
            w_bf = hy_w_in[j].astype(BF16)
            conv = (hy_b_in[j], hy_conv_w[j], hy_conv_b[j])
            x0_l, u_l = _hyena_in(x_parts[0], mod, i, w_bf, *conv, seq_len=len_latent, row0=0, tm=tm)
            x0_c, u_c = _hyena_in(x_parts[1], mod, i, w_bf, *conv, seq_len=len_ctx, row0=n_latent_rows, tm=tm)
            filt = (hy_f_w1[j], hy_f_b1[j], hy_f_w2[j], hy_f_b2[j], hy_f_w3[j], hy_f_freq[j], hy_decay[j])
            spec_lat = _filter_spectra(nb_latent, len_latent, CONV_BLOCK_LATENT, *filt, dft_lat[0])
            spec_ctx = _filter_spectra(1, len_ctx, CONV_BLOCK_CTX, *filt, dft_ctx[0])
            a_parts = [
                _hyena_conv(u_l, x0_l, spec_lat, hy_d[j], dft_lat[0], dft_lat[1], block=CONV_BLOCK_LATENT,
                            n_seq=1, nb=nb_latent),
                _hyena_conv(u_c, x0_c, spec_ctx, hy_d[j], dft_ctx[0], dft_ctx[1], block=CONV_BLOCK_CTX,
                            n_seq=tm // len_ctx, nb=1),
            ]
            w_o, b_o = hy_w_out[j], hy_b_out[j]
        else:
            if len(x_parts) == 2:
                x_parts = [jnp.concatenate(x_parts, axis=0)]
            qkv, nk, nv = _qkv(x_parts[0], mod, i, na_w_qkv[j], na_b_qkv[j], n_latent_rows=n_latent_rows,
                               n_ctx_batch=n_ctx_batch, len_ctx=len_ctx)
            new_k_layers.append(nk)
            new_v_layers.append(nv)
            tab = _nbr_bias_tiles(na_rpb[j])
            a_parts = [
                _nbr_attn(qkv, cache_k, cache_v, tab, j, n_batch=n_lat_batch, len_latent=len_latent),
                _ctx_attn(qkv, row_block0=n_latent_rows // len_ctx, n_batch=n_ctx_batch, len_ctx=len_ctx),
            ]
            w_o, b_o = na_w_out[j], na_b_out[j]
        x, tb, route, counts = _out_proj_route(a_parts, x_parts, w_o, b_o, mod, i, ln_g[i, 0], ln_b[i, 0],
                                               alpha, moe_w_group[i], moe_b_group[i], moe_w_expert[i],
                                               moe_b_expert[i])
        split = n_latent_rows if i == depth - 1 else None
        x = _moe_layer(x, tb, route, counts, mod, i, moe_w_up, moe_w_down, ln_g[i, 1], ln_b[i, 1], alpha,
                       split_rows=split)
        x_parts = list(x) if split is not None else [x]

    y_sample = x_parts[0].reshape(n_lat_batch, len_latent, d)
    y_prompt = x_parts[1].reshape(n_ctx_batch, len_ctx, d)
    new_k = jnp.concatenate(new_k_layers, axis=1)
    new_v = jnp.concatenate(new_v_layers, axis=1)
    return (y_prompt, y_sample, new_k, new_v)
```

```python
import functools
import math

import numpy as np
import jax
import jax.numpy as jnp
from jax import lax
from jax.experimental import pallas as pl
from jax.experimental.pallas import tpu as pltpu

F32 = jnp.float32
BF16 = jnp.bfloat16
HIGHEST = lax.Precision.HIGHEST

GRID_W = 64
N_BANDS = 16
N_HEADS = 16
WIN_ROWS = 8
WIN_COLS = 16
N_GROUPS = 4
EXPERTS_PER_GROUP = 4
LN_EPS = 1e-5
LOG2_E = 1.4426950408889634

LANES = 128
SUBLANES = 8
BF16_ROWS = 16
VMEM_LIMIT_BYTES = 56 * 1024 * 1024

FILTER_PAD = 128
CONV_BLOCK_CTX = 256
CONV_BLOCK_LATENT = 512

MOE_TOKEN_TILE = 512
MOE_ROW_TILE = 512
NBR_QUERY_ROWS = 4


def _cparams(sem):
    return pltpu.CompilerParams(dimension_semantics=sem, vmem_limit_bytes=VMEM_LIMIT_BYTES)


def _dot(a, b, precision=None):
    return jnp.dot(a, b, preferred_element_type=F32, precision=precision)


def _dot_nt(a, b, precision=None):
    return lax.dot_general(a, b, (((1,), (1,)), ((), ())), preferred_element_type=F32,
                           precision=precision)


def _dot3(a, b):
    a_hi = a.astype(BF16)
    a_lo = (a - a_hi.astype(F32)).astype(BF16)
    b_hi = b.astype(BF16)
    b_lo = (b - b_hi.astype(F32)).astype(BF16)
    return _dot(a_hi, b_hi) + (_dot(a_hi, b_lo) + _dot(a_lo, b_hi))


def _silu(x):
    return x / (1.0 + jnp.exp(-x))


def _layer_norm(r, g, b):
    mu = jnp.mean(r, axis=-1, keepdims=True)
    d = r - mu
    var = jnp.mean(d * d, axis=-1, keepdims=True)
    return d * lax.rsqrt(var + LN_EPS) * g + b


def _round_up(x, m):
    return (x + m - 1) // m * m


def _adaln_body(c_ref, w_ref, b_ref, o_ref):
    o_ref[0] = _dot(_silu(c_ref[...]), w_ref[0], HIGHEST) + b_ref[0]


def _adaln(cond, w_ada, b_ada):
    depth, d, d6 = w_ada.shape
    n_cond = cond.shape[0]
    tn = 1024
    return pl.pallas_call(
        _adaln_body,
        grid=(depth, d6 // tn),
        in_specs=[
            pl.BlockSpec((n_cond, d), lambda l, j: (0, 0)),
            pl.BlockSpec((1, d, tn), lambda l, j: (l, 0, j)),
            pl.BlockSpec((1, 1, tn), lambda l, j: (l, 0, j)),
        ],
        out_specs=pl.BlockSpec((1, n_cond, tn), lambda l, j: (l, 0, j)),
        out_shape=jax.ShapeDtypeStruct((depth, n_cond, d6), F32),
        compiler_params=_cparams(("arbitrary", "arbitrary")),
        name="adaln",
    )(cond, w_ada, b_ada.reshape(depth, 1, d6))


class _Mod:
    def __init__(self, table, n_cond, rows_per_cond, n_latent_cond):
        self.table = table
        self.n_cond = n_cond
        self.rows_per_cond = rows_per_cond
        self.n_latent_cond = n_latent_cond

    def spec(self, layer, which, tm, row0=0):
        base = (layer * 6 + which) * self.n_cond
        rpc, nl = self.rows_per_cond, self.n_latent_cond
        d = self.table.shape[-1]

        def index_map(i, *_):
            return (base + jnp.minimum((row0 + i * tm) // rpc, nl), 0, 0)

        return pl.BlockSpec((1, 1, d), index_map)


def _hyena_in_body(x_ref, sh_ref, sc_ref, w0_ref, w1_ref, w2_ref, b0_ref, b1_ref, b2_ref,
                   cw0_ref, cw1_ref, cw2_ref, cb0_ref, cb1_ref, cb2_ref,
                   x0_ref, u_ref, h_ref, *, seq_len):
    j = pl.program_id(1)
    tm = x_ref.shape[0]

    @pl.when(j == 0)
    def _():
        h_ref[...] = (x_ref[...] * (1.0 + sc_ref[0]) + sh_ref[0]).astype(BF16)

    pos =lax.broadcasted_iota(jnp.int32, (tm, 1), 0) & (seq_len - 1)
    first = pos == 0
    last = pos == seq_len - 1

    def part(w_ref, b_ref, cw_ref, cb_ref):
        z = _dot(h_ref[...], w_ref[...]) + b_ref[...]
        z_prev = jnp.where(first, 0.0, pltpu.roll(z, 1, 0))
        z_next = jnp.where(last, 0.0, pltpu.roll(z, tm - 1, 0))
        cw = cw_ref[...]
        return cb_ref[...] + z_prev * cw[0:1] + z * cw[1:2] + z_next * cw[2:3]

    x0_ref[...] = part(w0_ref, b0_ref, cw0_ref, cb0_ref).astype(x0_ref.dtype)
    x1 = part(w1_ref, b1_ref, cw1_ref, cb1_ref)
    v = part(w2_ref, b2_ref, cw2_ref, cb2_ref)
    u_ref[...] = (v * x1).astype(u_ref.dtype)


def _hyena_in(x, mod, layer, w_bf, b_in, conv_w, conv_b, *, seq_len, row0, tm):
    t, d = x.shape
    tn = 256
    assert t % tm == 0 and tm % seq_len == 0 and seq_len & (seq_len - 1) == 0
    nj = d // tn
    b2d = b_in.reshape(1, 3 * d)
    cb2d = conv_b.reshape(1, 3 * d)

    def col(p):
        return lambda i, j: (0, p * nj + j)

    out = jax.ShapeDtypeStruct((t, d), BF16)
    return pl.pallas_call(
        functools.partial(_hyena_in_body, seq_len=seq_len),
        grid=(t // tm, nj),
        in_specs=[pl.BlockSpec((tm, d), lambda i, j: (i, 0)),
                  mod.spec(layer, 0, tm, row0), mod.spec(layer, 1, tm, row0)]
        + [pl.BlockSpec((d, tn), col(p)) for p in range(3)]
        + [pl.BlockSpec((1, tn), col(p)) for p in range(3)]
        + [pl.BlockSpec((3, tn), col(p)) for p in range(3)]
        + [pl.BlockSpec((1, tn), col(p)) for p in range(3)],
        out_specs=[pl.BlockSpec((tm, tn), lambda i, j: (i, j))] * 2,
        out_shape=[out, out],
        scratch_shapes=[pltpu.VMEM((tm, d), BF16)],
        compiler_params=_cparams(("arbitrary", "arbitrary")),
        name=f"hyena_in_{seq_len}",
    )(x, mod.table, mod.table, w_bf, w_bf, w_bf, b2d, b2d, b2d,
      conv_w, conv_w, conv_w, cb2d, cb2d, cb2d)


def _freq_pad(block):
    return _round_up(block + 1, SUBLANES)


def _dft_matrices(block):
    n = 2 * block
    nf = block + 1
    fp = _freq_pad(block)
    f = np.arange(nf, dtype=np.float64)[:, None]
    m = np.arange(n, dtype=np.float64)[None, :]
    ang = 2.0 * np.pi * f * m / n
    fwd = np.zeros((2 * fp, n), np.float64)
    fwd[:nf] = np.cos(ang)
    fwd[fp:fp + nf] = -np.sin(ang)
    a = np.arange(block, dtype=np.float64)[:, None]
    fr = np.arange(nf, dtype=np.float64)[None, :]
    ang_i = 2.0 * np.pi * a * fr / n
    weight = np.full((1, nf), 2.0)
    weight[0, 0] = 1.0
    weight[0, nf - 1] = 1.0
    inv = np.zeros((block, 2 * fp), np.float64)
    inv[:, :nf] = weight * np.cos(ang_i) / n
    inv[:, fp:fp + nf] = -weight * np.sin(ang_i) / n
    return fwd.astype(np.float32), inv.astype(np.float32)


def _lag_tables(nb, seq_len, block):
    n = 2 * block
    lag = np.zeros((2 * nb - 1, n, 1), np.float32)
    valid = np.zeros((2 * nb - 1, n, 1), np.float32)
    m = np.arange(n)
    for dd in range(2 * nb - 1):
        delta = dd - (nb - 1)
        lg = np.where(m < block, block * delta + m, block * delta + m - n)
        ok = (m != block) & (np.abs(lg) <= seq_len - 1)
        lag[dd, :, 0] = np.where(ok, lg, 0)
        valid[dd, :, 0] = ok
    return lag, valid


def _filter_body(lag_ref, valid_ref, bands_ref, w1t_ref, w1c_ref, w1s_ref, b1_ref, w2_ref, b2_ref,
                 freq_ref, w3_ref, dec_ref, dft_ref, k_ref, *, seq_len):
    d = k_ref.shape[-1]
    lag = lag_ref[0]
    pos = jnp.abs(lag)
    tt = pos / float(max(seq_len - 1, 1))
    ang = (2.0 * math.pi / seq_len) * pos * bands_ref[...]
    freq = freq_ref[...]
    pre = (tt * w1t_ref[...] + _dot(jnp.cos(ang), w1c_ref[...], HIGHEST)
           + _dot(-jnp.sin(ang), w1s_ref[...], HIGHEST) + b1_ref[...])
    h = jnp.sin(freq * pre)
    h = jnp.sin(freq * (_dot(h, w2_ref[...], HIGHEST) + b2_ref[...]))
    block = lag.shape[0] // 2
    delta = pl.program_id(0) - (pl.num_programs(0) - 1) // 2

    def half(rows, forward):
        w3 = jnp.where(forward, w3_ref[:, :d], w3_ref[:, d:])
        dec = jnp.where(forward, dec_ref[:, :d], dec_ref[:, d:])
        return _dot3(h[rows], w3) * jnp.exp(-tt[rows] * jnp.abs(dec))

    taps = jnp.concatenate([half(slice(0, block), delta >= 0), half(slice(block, 2 * block), delta >= 1)],
                           axis=0)
    taps = jnp.where(valid_ref[0] > 0.5, taps, 0.0)
    k_ref[0] = _dot3(dft_ref[...], taps)


def _filter_spectra(nb, seq_len, block, f_w1, f_b1, f_w2, f_b2, f_w3, f_freq, decay, dft_fwd):
    hid = f_w1.shape[1]
    d2 = f_w3.shape[1]
    d = d2 // 2
    n = 2 * block
    fp2 = dft_fwd.shape[0]
    lag, valid = _lag_tables(nb, seq_len, block)
    ph = FILTER_PAD - hid
    bands = np.zeros((1, LANES), np.float32)
    bands[0, :N_BANDS] = np.linspace(1e-4, N_BANDS - 1, N_BANDS, dtype=np.float32)
    w1t = jnp.pad(f_w1[0:1], ((0, 0), (0, ph)))
    w1c = jnp.pad(f_w1[1:1 + N_BANDS], ((0, LANES - N_BANDS), (0, ph)))
    w1s = jnp.pad(f_w1[1 + N_BANDS:1 + 2 * N_BANDS], ((0, LANES - N_BANDS), (0, ph)))
    b1 = jnp.pad(f_b1.reshape(1, hid), ((0, 0), (0, ph)))
    w2 = jnp.pad(f_w2, ((0, ph), (0, ph)))
    b2 = jnp.pad(f_b2.reshape(1, hid), ((0, 0), (0, ph)))
    freq = jnp.pad(f_freq.reshape(1, hid), ((0, 0), (0, ph)))
    w3 = jnp.pad(f_w3, ((0, ph), (0, 0)))
    dec = decay.reshape(1, d2)
    nd = 2 * nb - 1

    def whole(shape):
        return pl.BlockSpec(shape, lambda dd: (0,) * len(shape))

    return pl.pallas_call(
        functools.partial(_filter_body, seq_len=seq_len),
        grid=(nd,),
        in_specs=[
            pl.BlockSpec((1, n, 1), lambda dd: (dd, 0, 0)),
            pl.BlockSpec((1, n, 1), lambda dd: (dd, 0, 0)),
            whole((1, LANES)), whole((1, FILTER_PAD)), whole((LANES, FILTER_PAD)),
            whole((LANES, FILTER_PAD)), whole((1, FILTER_PAD)), whole((FILTER_PAD, FILTER_PAD)),
            whole((1, FILTER_PAD)), whole((1, FILTER_PAD)),
            whole((FILTER_PAD, d2)), whole((1, d2)), whole((fp2, n)),
        ],
        out_specs=pl.BlockSpec((1, fp2, d), lambda dd: (dd, 0, 0)),
        out_shape=jax.ShapeDtypeStruct((nd, fp2, d), F32),
        compiler_params=_cparams(("arbitrary",)),
        name=f"hyena_filter_{seq_len}",
    )(jnp.asarray(lag), jnp.asarray(valid), jnp.asarray(bands), w1t, w1c, w1s, b1, w2, b2, freq,
      w3, dec, dft_fwd)


def _hyena_conv_body(u_ref, x0_ref, k_ref, dsk_ref, dftu_ref, idft_ref, o_ref, uf_ref, yf_ref,
                     *, n_seq, nb, block):
    fp = k_ref.shape[1] // 2
    ft_rows = SUBLANES

    def fwd(blk, carry):
        rows = pl.ds(pl.multiple_of(blk * block, block), block)
        uf_ref[blk] = _dot(dftu_ref[...], u_ref[rows, :])
        return carry

    lax.fori_loop(0, n_seq * nb, fwd, 0)

    def freq_tile(ft, carry):
        r = pl.multiple_of(ft * ft_rows, ft_rows)
        re = pl.ds(r, ft_rows)
        im = pl.ds(fp + r, ft_rows)
        k_re = [k_ref[dd, re, :] for dd in range(2 * nb - 1)]
        k_im = [k_ref[dd, im, :] for dd in range(2 * nb - 1)]
        for s in range(n_seq):
            u_re = [uf_ref[s * nb + j, re, :] for j in range(nb)]
            u_im = [uf_ref[s * nb + j, im, :] for j in range(nb)]
            for i in range(nb):
                acc_re = None
                acc_im = None
                for j in range(nb):
                    dd = i - j + (nb - 1)
                    t_re = k_re[dd] * u_re[j] - k_im[dd] * u_im[j]
                    t_im = k_re[dd] * u_im[j] + k_im[dd] * u_re[j]
                    acc_re = t_re if acc_re is None else acc_re + t_re
                    acc_im = t_im if acc_im is None else acc_im + t_im
                yf_ref[s * nb + i, re, :] = acc_re
                yf_ref[s * nb + i, im, :] = acc_im
        return carry

    lax.fori_loop(0, fp // ft_rows, freq_tile, 0)

    def inv(blk, carry):
        rows = pl.ds(pl.multiple_of(blk * block, block), block)
        y = _dot(idft_ref[...], yf_ref[blk].astype(BF16))
        u_blk = u_ref[rows, :].astype(F32)
        o_ref[rows, :] = ((y + u_blk * dsk_ref[...]) * x0_ref[rows, :].astype(F32)).astype(o_ref.dtype)
        return carry

    lax.fori_loop(0, n_seq * nb, inv, 0)


def _hyena_conv(u, x0, spectra, d_skip, dft_fwd, dft_inv, *, block, n_seq, nb):
    t, d = u.shape
    ct = 256
    tm = n_seq * nb * block
    nd = 2 * nb - 1
    fp2 = dft_fwd.shape[0]
    rows = lambda c, b: (b, c)
    return pl.pallas_call(
        functools.partial(_hyena_conv_body, n_seq=n_seq, nb=nb, block=block),
        grid=(d // ct, t // tm),
        in_specs=[
            pl.BlockSpec((tm, ct), rows),
            pl.BlockSpec((tm, ct), rows),
            pl.BlockSpec((nd, fp2, ct), lambda c, b: (0, 0, c)),
            pl.BlockSpec((1, ct), lambda c, b: (0, c)),
            pl.BlockSpec((fp2, block), lambda c, b: (0, 0)),
            pl.BlockSpec((block, fp2), lambda c, b: (0, 0)),
        ],
        out_specs=pl.BlockSpec((tm, ct), rows),
        out_shape=jax.ShapeDtypeStruct((t, d), BF16),
        scratch_shapes=[pltpu.VMEM((n_seq * nb, fp2, ct), F32),
                        pltpu.VMEM((n_seq * nb, fp2, ct), F32)],
        compiler_params=_cparams(("arbitrary", "arbitrary")),
        name=f"hyena_conv_nb{nb}",
    )(u, x0, spectra, d_skip.reshape(1, d), dft_fwd[:, :block].astype(BF16), dft_inv.astype(BF16))


def _out_proj_body(*refs, alpha, split_tile, x_parts):
    a_refs = refs[0:2]
    x_refs = refs[2:2 + x_parts]
    (w_ref, b_ref, gate_ref, g_ref, beta_ref, sh_ref, sc_ref, rw_ref, rb_ref,
     o_ref, t_ref, route_ref, cnt_ref) = refs[2 + x_parts:]
    i = pl.program_id(0)

    def run(a_ref, x_ref):
        o = _dot(a_ref[...], w_ref[...]) + b_ref[...]
        r = alpha * x_ref[...] + gate_ref[0] * o
        o_ref[...] = _layer_norm(r, g_ref[...], beta_ref[...])

    @pl.when(i < split_tile)
    def _():
        run(a_refs[0], x_refs[0])

    @pl.when(i >= split_tile)
    def _():
        run(a_refs[1], x_refs[-1])

    t = o_ref[...] * (1.0 + sc_ref[0]) + sh_ref[0]
    t_ref[...] = t.astype(BF16)
    route, counts = _route_tokens(t, rw_ref[...], rb_ref[...])
    route_ref[...] = route
    cnt_ref[0] = jnp.broadcast_to(counts, cnt_ref.shape[1:])


def _out_proj_route(a_parts, x_parts, w, b, mod, layer, ln_g, ln_b, alpha, w_group, b_group, w_expert,
                    b_expert):
    d = a_parts[0].shape[1]
    tm = MOE_TOKEN_TILE
    split_tile = a_parts[0].shape[0] // tm
    t = a_parts[0].shape[0] + a_parts[1].shape[0]
    pad = LANES - N_GROUPS - w_expert.shape[1]
    rw = jnp.pad(jnp.concatenate([w_group, w_expert], axis=1), ((0, 0), (0, pad)))
    rb = jnp.pad(jnp.concatenate([b_group, b_expert]), (0, pad)).reshape(1, LANES)
    first = lambda i: (jnp.minimum(i, split_tile - 1), 0)
    second = lambda i: (jnp.maximum(i - split_tile, 0), 0)
    row = lambda i: (i, 0)
    const = lambda i: (0, 0)
    part_specs = [pl.BlockSpec((tm, d), first), pl.BlockSpec((tm, d), second)]
    x_specs = part_specs if len(x_parts) == 2 else [pl.BlockSpec((tm, d), row)]
    return pl.pallas_call(
        functools.partial(_out_proj_body, alpha=alpha, split_tile=split_tile, x_parts=len(x_parts)),
        grid=(t // tm,),
        in_specs=part_specs + x_specs
        + [pl.BlockSpec((d, d), const), pl.BlockSpec((1, d), const), mod.spec(layer, 2, tm),
           pl.BlockSpec((1, d), const), pl.BlockSpec((1, d), const),
           mod.spec(layer, 3, tm), mod.spec(layer, 4, tm),
           pl.BlockSpec((d, LANES), const), pl.BlockSpec((1, LANES), const)],
        out_specs=[pl.BlockSpec((tm, d), row), pl.BlockSpec((tm, d), row), pl.BlockSpec((tm, LANES), row),
                   pl.BlockSpec((1, SUBLANES, LANES), lambda i: (i, 0, 0))],
        out_shape=[jax.ShapeDtypeStruct((t, d), F32), jax.ShapeDtypeStruct((t, d), BF16),
                   jax.ShapeDtypeStruct((t, LANES), F32),
                   jax.ShapeDtypeStruct((t // tm, SUBLANES, LANES), F32)],
        compiler_params=_cparams(("arbitrary",)),
        name="out_proj_ln_route",
    )(*a_parts, *x_parts, w.astype(BF16), b.reshape(1, d), mod.table,
      ln_g.reshape(1, d), ln_b.reshape(1, d), mod.table, mod.table, rw, rb)


def _route_tokens(t, w, b):
    tm = t.shape[0]
    lg = _dot3(t, w) + b
    lane = lax.broadcasted_iota(jnp.int32, lg.shape, 1)
    neg = -jnp.inf
    far = jnp.int32(LANES)
    g_mask = lane < N_GROUPS
    gl = jnp.where(g_mask, lg, neg)
    g_max = jnp.max(gl, axis=-1, keepdims=True)
    g_sel = jnp.min(jnp.where(gl == g_max, lane, far), axis=-1, keepdims=True)
    g_w = 1.0 / jnp.sum(jnp.where(g_mask, jnp.exp(lg - g_max), 0.0), axis=-1, keepdims=True)
    e_lo = N_GROUPS + EXPERTS_PER_GROUP * g_sel
    e_mask = (lane >= e_lo) & (lane < e_lo + EXPERTS_PER_GROUP)
    el = jnp.where(e_mask, lg, neg)
    m1 = jnp.max(el, axis=-1, keepdims=True)
    i1 = jnp.min(jnp.where(el == m1, lane, far), axis=-1, keepdims=True)
    el2 = jnp.where(lane == i1, neg, el)
    m2 = jnp.max(el2, axis=-1, keepdims=True)
    i2 = jnp.min(jnp.where(el2 == m2, lane, far), axis=-1, keepdims=True)
    ratio = jnp.exp(m2 - m1)
    w1 = 1.0 / (1.0 + ratio)
    w2 = ratio / (1.0 + ratio)
    e1 = i1 - N_GROUPS
    e2 = i2 - N_GROUPS
    hit = (lane == e1) | (lane == e2)
    counts = jnp.sum(hit.astype(F32), axis=0, keepdims=True)
    run_len = jnp.floor((counts + (BF16_ROWS - 1)) * (1.0 / BF16_ROWS)) * BF16_ROWS
    lower = (lax.broadcasted_iota(jnp.int32, (LANES, LANES), 0)
             < lax.broadcasted_iota(jnp.int32, (LANES, LANES), 1)).astype(F32)
    starts = _dot(jnp.broadcast_to(run_len, (SUBLANES, LANES)), lower, HIGHEST)[0:1]
    earlier = (lax.broadcasted_iota(jnp.int32, (tm, tm), 1)
               < lax.broadcasted_iota(jnp.int32, (tm, tm), 0)).astype(BF16)
    pos = _dot(earlier, hit.astype(BF16)) + starts
    p1 = jnp.sum(jnp.where(lane == e1, pos, 0.0), axis=-1, keepdims=True)
    p2 = jnp.sum(jnp.where(lane == e2, pos, 0.0), axis=-1, keepdims=True)
    fields = (e1.astype(F32), e2.astype(F32), g_w * w1, g_w * w2, p1, p2)
    route = jnp.zeros(lg.shape, F32)
    for k, val in enumerate(fields):
        route = jnp.where(lane == k, val, route)
    return route, counts


def _moe_layout(counts, n_experts, n_row_tiles_max, chunk_slots):
    cnt = counts[:, 0, :n_experts].astype(jnp.int32)
    run_len = (cnt + (BF16_ROWS - 1)) // BF16_ROWS * BF16_ROWS
    total = jnp.sum(run_len, axis=0)
    region = (total + (MOE_ROW_TILE - 1)) // MOE_ROW_TILE * MOE_ROW_TILE
    region_end = jnp.cumsum(region)
    expert_start = region_end - region
    run_start = expert_start[None, :] + jnp.cumsum(run_len, axis=0) - run_len
    n_used = (region_end[-1] // MOE_ROW_TILE).reshape(1).astype(jnp.int32)
    tile_idx = jnp.arange(n_row_tiles_max, dtype=jnp.int32)
    tile_expert = jnp.minimum(
        jnp.sum((tile_idx[:, None] >= (region_end // MOE_ROW_TILE)[None, :]).astype(jnp.int32), axis=1),
        n_experts - 1).astype(jnp.int32)
    used_rows = region_end[-1:]
    tail_chunks = (n_row_tiles_max * MOE_ROW_TILE - used_rows) // (MOE_ROW_TILE // 2)
    local_end = jnp.cumsum(run_len, axis=1)
    pos = jnp.arange(chunk_slots, dtype=jnp.int32) * BF16_ROWS
    run_of_chunk = jnp.sum((pos[None, :, None] >= local_end[:, None, :]).astype(jnp.int32), axis=-1)
    in_run = run_of_chunk[:, :, None] == jnp.arange(n_experts, dtype=jnp.int32)[None, None, :]
    shift = (run_start - (local_end - run_len))[:, None, :]
    chunk_row = jnp.sum(jnp.where(in_run, shift, 0), axis=-1) + pos[None, :]
    chunk_row = jnp.where(pos[None, :] < local_end[:, -1:], chunk_row, 0)
    return dict(chunk_row=chunk_row.reshape(-1).astype(jnp.int32),
                n_chunks=(local_end[:, -1] // BF16_ROWS).astype(jnp.int32),
                gap_start=jnp.concatenate([expert_start + total, used_rows]).astype(jnp.int32),
                gap_len=jnp.concatenate([region - total, tail_chunks]).astype(jnp.int32),
                tile_expert=tile_expert, n_used=n_used)


def _run_chunk_loops(chunk_copy, n_chunks):
    def start(c, carry):
        chunk_copy(c).start()
        return carry

    def wait(c, carry):
        chunk_copy(c).wait()
        return carry

    return (lambda: lax.fori_loop(0, n_chunks, start, 0)), (lambda: lax.fori_loop(0, n_chunks, wait, 0))


def _chunk_copies(src_ref, src_off, dst_ref, dst_off, n, sem, max_chunk, advance_src=True):
    out = []
    off = jnp.int32(0)
    bit = max_chunk
    while bit >= BF16_ROWS:
        take = n & bit
        s = pl.multiple_of(src_off + off, BF16_ROWS) if advance_src else src_off
        t = pl.multiple_of(dst_off + off, BF16_ROWS)
        cp = pltpu.make_async_copy(src_ref.at[pl.ds(s, bit)], dst_ref.at[pl.ds(t, bit)], sem)
        out.append((take != 0, cp))
        off = off + take
        bit //= 2
    return out


def _start_all(copies):
    for cond, cp in copies:
        pl.when(cond)(cp.start)


def _wait_all(copies):
    for cond, cp in copies:
        pl.when(cond)(cp.wait)


def _dispatch_body(cd_ref, nch_ref, gs_ref, gl_ref, t_ref, route_ref, xg_ref, sorted_ref, zero_ref, sems,
                   *, n_experts):
    tile = pl.program_id(0)
    last_tile = pl.num_programs(0) - 1
    buf = lax.rem(tile, 2)
    sem = sems.at[0]
    tm = t_ref.shape[0]
    n_rows = sorted_ref.shape[1]
    slots = n_rows // BF16_ROWS
    route = route_ref[...]
    lane = lax.broadcasted_iota(jnp.int32, (tm, LANES), 1)
    both = jnp.where(lane == 0, route[:, 4:5], jnp.where(lane == 1, route[:, 5:6], 0.0))
    pick = (lax.broadcasted_iota(jnp.int32, (SUBLANES, LANES), 0)
            == lax.broadcasted_iota(jnp.int32, (SUBLANES, LANES), 1)).astype(F32)
    as_rows = _dot_nt(pick, both, HIGHEST)
    r_iota = lax.broadcasted_iota(jnp.int32, (n_rows, tm), 0).astype(F32)
    select = ((r_iota == as_rows[0:1]) | (r_iota == as_rows[1:2])).astype(BF16)
    sorted_ref[buf] = _dot(select, t_ref[...]).astype(BF16)

    def chunk_loops(tl, b):
        def chunk_copy(c):
            src = pl.multiple_of(c * BF16_ROWS, BF16_ROWS)
            dst = pl.multiple_of(cd_ref[tl * slots + c], BF16_ROWS)
            return pltpu.make_async_copy(sorted_ref.at[b, pl.ds(src, BF16_ROWS)],
                                         xg_ref.at[pl.ds(dst, BF16_ROWS)], sems.at[b])

        return _run_chunk_loops(chunk_copy, nch_ref[tl])

    start_chunks, wait_chunks = chunk_loops(tile, buf)
    start_chunks()

    @pl.when(tile > 0)
    def _():
        chunk_loops(tile - 1, 1 - buf)[1]()

    @pl.when(tile == last_tile)
    def _():
        wait_chunks()
        zero_ref[...] = jnp.zeros_like(zero_ref)
        fills = []
        for e in range(n_experts):
            fills += _chunk_copies(zero_ref, 0, xg_ref, gs_ref[e], gl_ref[e], sem, zero_ref.shape[0],
                                   advance_src=False)
        _start_all(fills)
        _wait_all(fills)
        z_rows = zero_ref.shape[0]
        tail0 = gs_ref[n_experts]

        def tail_copy(c):
            dst = pl.multiple_of(tail0 + c * z_rows, z_rows)
            return pltpu.make_async_copy(zero_ref, xg_ref.at[pl.ds(dst, z_rows)], sem)

        def start(c, carry):
            tail_copy(c).start()
            return carry

        def wait(c, carry):
            tail_copy(c).wait()
            return carry

        lax.fori_loop(0, gl_ref[n_experts], start, 0)
        lax.fori_loop(0, gl_ref[n_experts], wait, 0)


def _sorted_rows(tm, n_experts):
    return _round_up(2 * tm + n_experts * (BF16_ROWS - 1), LANES)


def _dispatch(tb, route, layout, n_experts, n_rows_max):
    t, d = tb.shape
    tm = MOE_TOKEN_TILE
    row = lambda i, *_: (i, 0)
    grid_spec = pltpu.PrefetchScalarGridSpec(
        num_scalar_prefetch=4,
        grid=(t // tm,),
        in_specs=[pl.BlockSpec((tm, d), row), pl.BlockSpec((tm, LANES), row)],
        out_specs=pl.BlockSpec(memory_space=pl.ANY),
        scratch_shapes=[pltpu.VMEM((2, _sorted_rows(tm, n_experts), d), BF16),
                        pltpu.VMEM((MOE_ROW_TILE // 2, d), BF16),
                        pltpu.SemaphoreType.DMA((2,))],
    )
    return pl.pallas_call(
        functools.partial(_dispatch_body, n_experts=n_experts),
        grid_spec=grid_spec,
        out_shape=jax.ShapeDtypeStruct((n_rows_max, d), BF16),
        compiler_params=_cparams(("arbitrary",)),
        name="moe_dispatch",
    )(layout["chunk_row"], layout["n_chunks"], layout["gap_start"], layout["gap_len"], tb, route)


def _experts_body(te_ref, nu_ref, x_ref, wu_ref, wd_ref, y_ref, wub_ref, wdb_ref, *, d_expert):
    g = pl.program_id(0)

    @pl.when(g < nu_ref[0])
    def _():
        prev = te_ref[jnp.maximum(g - 1, 0)]

        @pl.when((g == 0) | (te_ref[g] != prev))
        def _():
            wub_ref[...] = wu_ref[0, 0].astype(BF16)
            wdb_ref[...] = wd_ref[0, 0].astype(BF16)

        ab = _dot(x_ref[...], wub_ref[...])
        h = _silu(ab[:, :d_expert]) * ab[:, d_expert:]
        y_ref[...] = _dot(h.astype(BF16), wdb_ref[...]).astype(y_ref.dtype)

    @pl.when(g >= nu_ref[0])
    def _():
        y_ref[...] = jnp.zeros_like(y_ref)


def _experts(xg, w_up, w_down, layer, layout):
    n_rows, d = xg.shape
    d_up = w_up.shape[-1]
    d_expert = d_up // 2
    tm = MOE_ROW_TILE
    used = lambda g, nu: jnp.maximum(jnp.minimum(g, nu[0] - 1), 0)
    grid_spec = pltpu.PrefetchScalarGridSpec(
        num_scalar_prefetch=2,
        grid=(n_rows // tm,),
        in_specs=[pl.BlockSpec((tm, d), lambda g, te, nu: (used(g, nu), 0)),
                  pl.BlockSpec((1, 1, d, d_up), lambda g, te, nu: (layer, te[used(g, nu)], 0, 0)),
                  pl.BlockSpec((1, 1, d_expert, d), lambda g, te, nu: (layer, te[used(g, nu)], 0, 0))],
        out_specs=pl.BlockSpec((tm, d), lambda g, te, nu: (g, 0)),
        scratch_shapes=[pltpu.VMEM((d, d_up), BF16), pltpu.VMEM((d_expert, d), BF16)],
    )
    return pl.pallas_call(
        functools.partial(_experts_body, d_expert=d_expert),
        grid_spec=grid_spec,
        out_shape=jax.ShapeDtypeStruct((n_rows, d), BF16),
        compiler_params=_cparams(("arbitrary",)),
        name="moe_experts",
    )(layout["tile_expert"], layout["n_used"], xg, w_up, w_down)


def _combine_body(cd_ref, nch_ref, yg_ref, route_ref, x_ref, gate_ref, g_ref, beta_ref, *rest,
                  alpha, split_tile):
    if split_tile is None:
        o_ref, ybuf_ref, sems = rest
    else:
        o_ref, o2_ref, ybuf_ref, sems = rest
    tile = pl.program_id(0)
    buf = lax.rem(tile, 2)
    tm = x_ref.shape[0]
    n_rows = ybuf_ref.shape[1]
    slots = n_rows // BF16_ROWS

    def chunk_loops(tl, b):
        def chunk_copy(c):
            src = pl.multiple_of(cd_ref[tl * slots + c], BF16_ROWS)
            dst = pl.multiple_of(c * BF16_ROWS, BF16_ROWS)
            return pltpu.make_async_copy(yg_ref.at[pl.ds(src, BF16_ROWS)],
                                         ybuf_ref.at[b, pl.ds(dst, BF16_ROWS)], sems.at[b])

        return _run_chunk_loops(chunk_copy, nch_ref[tl])

    def fetch(tl, b):
        ybuf_ref[b, pl.ds(2 * tm, n_rows - 2 * tm), :] = jnp.zeros(
            (n_rows - 2 * tm, ybuf_ref.shape[2]), ybuf_ref.dtype)
        chunk_loops(tl, b)[0]()

    @pl.when(tile == 0)
    def _():
        fetch(tile, buf)

    @pl.when(tile + 1 < pl.num_programs(0))
    def _():
        fetch(tile + 1, 1 - buf)

    route = route_ref[...]
    r_lane = lax.broadcasted_iota(jnp.int32, (tm, n_rows), 1).astype(F32)
    cmat = (jnp.where(r_lane == route[:, 4:5], route[:, 2:3], 0.0)
            + jnp.where(r_lane == route[:, 5:6], route[:, 3:4], 0.0)).astype(BF16)
    chunk_loops(tile, buf)[1]()
    y = _dot(cmat, ybuf_ref[buf])
    r = alpha * x_ref[...] + gate_ref[0] * y
    res = _layer_norm(r, g_ref[...], beta_ref[...])
    if split_tile is None:
        o_ref[...] = res
    else:
        @pl.when(tile < split_tile)
        def _():
            o_ref[...] = res

        @pl.when(tile >= split_tile)
        def _():
            o2_ref[...] = res


def _combine(yg, route, x, layout, mod, layer, ln_g, ln_b, alpha, n_experts, split_rows=None):
    t, d = x.shape
    tm = MOE_TOKEN_TILE
    row = lambda i, *_: (i, 0)
    const = lambda i, *_: (0, 0)
    if split_rows is None:
        split_tile = None
        out_specs = pl.BlockSpec((tm, d), row)
        out_shape = jax.ShapeDtypeStruct((t, d), F32)
    else:
        assert split_rows % tm == 0
        split_tile = split_rows // tm
        out_specs = [pl.BlockSpec((tm, d), lambda i, *_: (jnp.minimum(i, split_tile - 1), 0)),
                     pl.BlockSpec((tm, d), lambda i, *_: (jnp.maximum(i - split_tile, 0), 0))]
        out_shape = [jax.ShapeDtypeStruct((split_rows, d), F32),
                     jax.ShapeDtypeStruct((t - split_rows, d), F32)]
    grid_spec = pltpu.PrefetchScalarGridSpec(
        num_scalar_prefetch=2,
        grid=(t // tm,),
        in_specs=[pl.BlockSpec(memory_space=pl.ANY), pl.BlockSpec((tm, LANES), row),
                  pl.BlockSpec((tm, d), row), mod.spec(layer, 5, tm),
                  pl.BlockSpec((1, d), const), pl.BlockSpec((1, d), const)],
        out_specs=out_specs,
        scratch_shapes=[pltpu.VMEM((2, _sorted_rows(tm, n_experts), d), BF16),
                        pltpu.SemaphoreType.DMA((2,))],
    )
    return pl.pallas_call(
        functools.partial(_combine_body, alpha=alpha, split_tile=split_tile),
        grid_spec=grid_spec,
        out_shape=out_shape,
        compiler_params=_cparams(("arbitrary",)),
        name="moe_combine_ln",
    )(layout["chunk_row"], layout["n_chunks"], yg, route, x, mod.table,
      ln_g.reshape(1, d), ln_b.reshape(1, d))


def _moe_layer(x, tb, route, counts, mod, layer, w_up, w_down, ln_g, ln_b, alpha, split_rows=None):
    t, d = x.shape
    n_e = w_up.shape[1]
    n_tok_tiles = t // MOE_TOKEN_TILE
    max_rows = 2 * t + n_tok_tiles * n_e * (BF16_ROWS - 1) + n_e * (MOE_ROW_TILE - BF16_ROWS)
    n_rows_max = _round_up(max_rows, MOE_ROW_TILE)
    layout = _moe_layout(counts, n_e, n_rows_max // MOE_ROW_TILE,
                         _sorted_rows(MOE_TOKEN_TILE, n_e) // BF16_ROWS)
    xg = _dispatch(tb, route, layout, n_e, n_rows_max)
    yg = _experts(xg, w_up, w_down, layer, layout)
    return _combine(yg, route, x, layout, mod, layer, ln_g, ln_b, alpha, n_e, split_rows)


def _qkv_body(y_ref, sh_ref, sc_ref, w_ref, b_ref, qkv_ref, nk_ref, nv_ref, *, n_latent_tiles, d):
    i = pl.program_id(0)
    h = (y_ref[...] * (1.0 + sc_ref[0]) + sh_ref[0]).astype(BF16)
    z = _dot(h, w_ref[...]) + b_ref[...]
    qkv_ref[...] = z.astype(BF16)

    @pl.when(i >= n_latent_tiles)
    def _():
        hd = nk_ref.shape[-1]
        seq = nk_ref.shape[3]
        for s in range(nk_ref.shape[0]):
            rows = slice(s * seq, (s + 1) * seq)
            for hh in range(nk_ref.shape[2]):
                nk_ref[s, 0, hh] = z[rows, d + hh * hd:d + (hh + 1) * hd]
                nv_ref[s, 0, hh] = z[rows, 2 * d + hh * hd:2 * d + (hh + 1) * hd]


def _qkv(y, mod, layer, w, b, *, n_latent_rows, n_ctx_batch, len_ctx):
    t, d = y.shape
    seq_per_tile = 2
    tm = seq_per_tile * len_ctx
    hd = d // N_HEADS
    assert n_latent_rows % tm == 0 and n_ctx_batch % seq_per_tile == 0
    n_lat = n_latent_rows // tm
    kv_map = lambda i: (jnp.maximum(i - n_lat, 0), 0, 0, 0, 0)
    kv_shape = jax.ShapeDtypeStruct((n_ctx_batch, 1, N_HEADS, len_ctx, hd), F32)
    return pl.pallas_call(
        functools.partial(_qkv_body, n_latent_tiles=n_lat, d=d),
        grid=(t // tm,),
        in_specs=[pl.BlockSpec((tm, d), lambda i: (i, 0)), mod.spec(layer, 0, tm), mod.spec(layer, 1, tm),
                  pl.BlockSpec((d, 3 * d), lambda i: (0, 0)), pl.BlockSpec((1, 3 * d), lambda i: (0, 0))],
        out_specs=[pl.BlockSpec((tm, 3 * d), lambda i: (i, 0)),
                   pl.BlockSpec((seq_per_tile, 1, N_HEADS, len_ctx, hd), kv_map),
                   pl.BlockSpec((seq_per_tile, 1, N_HEADS, len_ctx, hd), kv_map)],
        out_shape=[jax.ShapeDtypeStruct((t, 3 * d), BF16), kv_shape, kv_shape],
        compiler_params=_cparams(("arbitrary",)),
        name="attn_qkv",
    )(y, mod.table, mod.table, w.astype(BF16), b.reshape(1, 3 * d))


def _ctx_attn_body(qkv_ref, o_ref, *, d, scale):
    pair = 2 * (d // N_HEADS)
    left = lax.broadcasted_iota(jnp.int32, (1, pair), 1) < pair // 2
    c = scale * LOG2_E
    for hp in range(d // pair):
        cols = slice(hp * pair, (hp + 1) * pair)
        q2 = qkv_ref[:, cols]
        k2 = qkv_ref[:, d + hp * pair:d + (hp + 1) * pair]
        v2 = qkv_ref[:, 2 * d + hp * pair:2 * d + (hp + 1) * pair]
        outs = []
        for mask in (left, ~left):
            s = _dot_nt(jnp.where(mask, q2, jnp.zeros_like(q2)), k2) * c
            e = jnp.exp2(s - jnp.max(s, axis=-1, keepdims=True))
            p = (e / jnp.sum(e, axis=-1, keepdims=True)).astype(BF16)
            outs.append(_dot(p, v2))
        o_ref[:, cols] = jnp.where(left, outs[0], outs[1]).astype(o_ref.dtype)


def _ctx_attn(qkv, *, row_block0, n_batch, len_ctx):
    d3 = qkv.shape[1]
    d = d3 // 3
    scale = (d // N_HEADS) ** -0.5
    return pl.pallas_call(
        functools.partial(_ctx_attn_body, d=d, scale=scale),
        grid=(n_batch,),
        in_specs=[pl.BlockSpec((len_ctx, d3), lambda b: (row_block0 + b, 0))],
        out_specs=pl.BlockSpec((len_ctx, d), lambda b: (b, 0)),
        out_shape=jax.ShapeDtypeStruct((n_batch * len_ctx, d), BF16),
        compiler_params=_cparams(("arbitrary",)),
        name="ctx_attention",
    )(qkv)


def _nbr_bias_body(rpb_ref, onehot_ref, band_ref, o_ref):
    picked = _dot(rpb_ref[...], onehot_ref[...], HIGHEST)
    o_ref[...] = jnp.where(band_ref[...] > 0.5, picked * LOG2_E, -jnp.inf)


def _nbr_bias_tiles(rpb):
    n_h, n_dr, n_dc = rpb.shape
    col = np.arange(GRID_W)
    col_start = np.clip(col - WIN_COLS // 2, 0, GRID_W - WIN_COLS)
    in_band = (col[None, :] >= col_start[:, None]) & (col[None, :] < col_start[:, None] + WIN_COLS)
    dc_idx = np.clip(col[None, :] - col[:, None], -(WIN_COLS - 1), WIN_COLS - 1) + (WIN_COLS - 1)
    n_pairs = GRID_W * GRID_W
    onehot = np.zeros((LANES, n_pairs), np.float32)
    onehot[dc_idx.reshape(-1), np.arange(n_pairs)] = 1.0
    band = in_band.reshape(1, n_pairs).astype(np.float32)
    rows = n_h * n_dr
    rpb2d = jnp.pad(rpb.reshape(rows, n_dc), ((0, 0), (0, LANES - n_dc)))
    whole = lambda shape: pl.BlockSpec(shape, lambda i: (0,) * len(shape))
    cols = pl.pallas_call(
        _nbr_bias_body,
        grid=(1,),
        in_specs=[whole((rows, LANES)), whole((LANES, n_pairs)), whole((1, n_pairs))],
        out_specs=whole((rows, n_pairs)),
        out_shape=jax.ShapeDtypeStruct((rows, n_pairs), F32),
        compiler_params=_cparams(("arbitrary",)),
        name="nbr_bias",
    )(rpb2d, jnp.asarray(onehot), jnp.asarray(band))
    cols = cols.reshape(n_h, n_dr, GRID_W, GRID_W)
    n_entries = 2 * WIN_ROWS
    masked = jnp.full((n_h, n_entries - n_dr, GRID_W, GRID_W), -jnp.inf, F32)
    cols = jnp.concatenate([cols, masked], axis=1)
    zeros = jnp.zeros_like(cols)
    left = jnp.concatenate([cols, zeros], axis=-1)
    right = jnp.concatenate([zeros, cols], axis=-1)
    return jnp.concatenate([left, right], axis=1)


def _nbr_attn_body(q_ref, k_ref, v_ref, ck_ref, cv_ref, tab_ref, o_ref, *, rows, kh, scale):
    hd = ck_ref.shape[-2]
    qr = NBR_QUERY_ROWS
    span = qr + kh
    n_q = qr * GRID_W
    n_k = span * GRID_W
    masked = 2 * WIN_ROWS - 1
    c = scale * LOG2_E
    left = lax.broadcasted_iota(jnp.int32, (1, 2 * hd), 1) < hd
    kc2_t = jnp.concatenate([ck_ref[0, 0, 0], ck_ref[0, 0, 1]], axis=0).astype(BF16)
    vc2_t = jnp.concatenate([cv_ref[0, 0, 0], cv_ref[0, 0, 1]], axis=0).astype(BF16)

    def block(qb):
        r0 = qb * qr
        s0 = jnp.clip(r0 - kh // 2, 0, rows - span)
        q_rows = pl.ds(pl.multiple_of(r0 * GRID_W, n_q), n_q)
        k_rows = pl.ds(pl.multiple_of(s0 * GRID_W, GRID_W), n_k)
        q2 = q_ref[q_rows, :]
        kw2 = k_ref[k_rows, :]
        vw2 = v_ref[k_rows, :]
        outs = []
        for hh, mask in enumerate((left, ~left)):
            bias_rows = []
            for rq in range(qr):
                r = r0 + rq
                ws = jnp.clip(r - kh // 2, 0, rows - kh)
                tiles = []
                for kp in range(span // 2):
                    ke = s0 + 2 * kp
                    ko = ke + 1
                    ie = jnp.where((ke >= ws) & (ke < ws + kh), ke - r + (WIN_ROWS - 1), masked)
                    io = jnp.where((ko >= ws) & (ko < ws + kh), ko - r + (WIN_ROWS - 1), masked)
                    tiles.append(tab_ref[hh, ie] + tab_ref[hh, 2 * WIN_ROWS + io])
                bias_rows.append(jnp.concatenate(tiles, axis=1))
            bias = jnp.concatenate(bias_rows, axis=0)
            qh = jnp.where(mask, q2, jnp.zeros_like(q2))
            s_loc = _dot_nt(qh, kw2) * c + bias
            s_ctx = _dot(qh, kc2_t) * c
            m = jnp.maximum(jnp.max(s_loc, axis=-1, keepdims=True), jnp.max(s_ctx, axis=-1, keepdims=True))
            e_loc = jnp.exp2(s_loc - m)
            e_ctx = jnp.exp2(s_ctx - m)
            denom = jnp.sum(e_loc, axis=-1, keepdims=True) + jnp.sum(e_ctx, axis=-1, keepdims=True)
            outs.append((_dot(e_loc.astype(BF16), vw2) + _dot_nt(e_ctx.astype(BF16), vc2_t)) / denom)
        o_ref[q_rows, :] = jnp.where(left, outs[0], outs[1]).astype(o_ref.dtype)

    def two_blocks(i, carry):
        block(2 * i)
        block(2 * i + 1)
        return carry

    lax.fori_loop(0, rows // qr // 2, two_blocks, 0)


def _nbr_attn(qkv, cache_k, cache_v, tab, layer_j, *, n_batch, len_latent):
    d3 = qkv.shape[1]
    t = n_batch * len_latent
    d = d3 // 3
    hd = d // N_HEADS
    rows = len_latent // GRID_W
    kh = min(WIN_ROWS, rows)
    assert rows % (2 * NBR_QUERY_ROWS) == 0 and rows >= NBR_QUERY_ROWS + kh and (NBR_QUERY_ROWS + kh) % 2 == 0
    assert 2 * hd == LANES
    n_hp = N_HEADS // 2
    pc = cache_k.shape[3]
    qmap = lambda p: (lambda b, h: (b, p * n_hp + h))
    cmap = lambda b, h: (b, layer_j, h, 0, 0)
    return pl.pallas_call(
        functools.partial(_nbr_attn_body, rows=rows, kh=kh, scale=hd ** -0.5),
        grid=(n_batch, n_hp),
        in_specs=[pl.BlockSpec((len_latent, 2 * hd), qmap(0)),
                  pl.BlockSpec((len_latent, 2 * hd), qmap(1)),
                  pl.BlockSpec((len_latent, 2 * hd), qmap(2)),
                  pl.BlockSpec((1, 1, 2, hd, pc), cmap),
                  pl.BlockSpec((1, 1, 2, hd, pc), cmap),
                  pl.BlockSpec((2,) + tab.shape[1:], lambda b, h: (h, 0, 0, 0))],
        out_specs=pl.BlockSpec((len_latent, 2 * hd), lambda b, h: (b, h)),
        out_shape=jax.ShapeDtypeStruct((t, d), BF16),
        compiler_params=_cparams(("arbitrary", "arbitrary")),
        name="nbr_attention",
    )(qkv, qkv, qkv, jnp.swapaxes(cache_k, 3, 4), jnp.swapaxes(cache_v, 3, 4), tab)


def kernel(x_prompt, x_sample, cache_k, cache_v, c, c_ctx, w_ada, b_ada, ln_g, ln_b, hy_w_in, hy_b_in, hy_conv_w, hy_conv_b, hy_f_w1, hy_f_b1, hy_f_w2, hy_f_b2, hy_f_w3, hy_f_freq, hy_decay, hy_d, hy_w_out, hy_b_out, na_w_qkv, na_b_qkv, na_rpb, na_w_out, na_b_out, moe_w_group, moe_b_group, moe_w_expert, moe_b_expert, moe_w_up, moe_w_down):
    n_ctx_batch, len_ctx, d = x_prompt.shape
    n_lat_batch, len_latent, _ = x_sample.shape
    depth = w_ada.shape[0]
    alpha = (2 * depth) ** 0.25
    n_latent_rows = n_lat_batch * len_latent
    n_ctx_rows = n_ctx_batch * len_ctx
    assert len_ctx == CONV_BLOCK_CTX and len_latent % CONV_BLOCK_LATENT == 0
    nb_latent = len_latent // CONV_BLOCK_LATENT

    x_parts = [x_sample.reshape(n_latent_rows, d), x_prompt.reshape(n_ctx_rows, d)]

    n_cond = 16
    cond = jnp.concatenate([c, c_ctx[None, :], jnp.zeros((n_cond - n_lat_batch - 1, d), F32)], axis=0)
    mod_raw = _adaln(cond, w_ada, b_ada)
    table = mod_raw.reshape(depth, n_cond, 6, d).transpose(0, 2, 1, 3).reshape(depth * 6 * n_cond, 1, d)
    mod = _Mod(table, n_cond, len_latent, n_lat_batch)

    dft_lat = [jnp.asarray(m) for m in _dft_matrices(CONV_BLOCK_LATENT)]
    dft_ctx = [jnp.asarray(m) for m in _dft_matrices(CONV_BLOCK_CTX)]

    new_k_layers, new_v_layers = [], []
    for i in range(depth):
        j = i // 2
        if i % 2 == 0:
            if len(x_parts) == 1:
                x_parts = [x_parts[0][:n_latent_rows], x_parts[0][n_latent_rows:]]
            tm = len_latent
            w_bf = hy_w_in[j].astype(BF16)
            conv = (hy_b_in[j], hy_conv_w[j], hy_conv_b[j])
            x0_l, u_l = _hyena_in(x_parts[0], mod, i, w_bf, *conv, seq_len=len_latent, row0=0, tm=tm)
            x0_c, u_c = _hyena_in(x_parts[1], mod, i, w_bf, *conv, seq_len=len_ctx, row0=n_latent_rows, tm=tm)
            filt = (hy_f_w1[j], hy_f_b1[j], hy_f_w2[j], hy_f_b2[j], hy_f_w3[j], hy_f_freq[j], hy_decay[j])
            spec_lat = _filter_spectra(nb_latent, len_latent, CONV_BLOCK_LATENT, *filt, dft_lat[0])
            spec_ctx = _filter_spectra(1, len_ctx, CONV_BLOCK_CTX, *filt, dft_ctx[0])
            a_parts = [
                _hyena_conv(u_l, x0_l, spec_lat, hy_d[j], dft_lat[0], dft_lat[1], block=CONV_BLOCK_LATENT,
                            n_seq=1, nb=nb_latent),
                _hyena_conv(u_c, x0_c, spec_ctx, hy_d[j], dft_ctx[0], dft_ctx[1], block=CONV_BLOCK_CTX,
                            n_seq=tm // len_ctx, nb=1),
            ]
            w_o, b_o = hy_w_out[j], hy_b_out[j]
        else:
            if len(x_parts) == 2:
                x_parts = [jnp.concatenate(x_parts, axis=0)]
            qkv, nk, nv = _qkv(x_parts[0], mod, i, na_w_qkv[j], na_b_qkv[j], n_latent_rows=n_latent_rows,
                               n_ctx_batch=n_ctx_batch, len_ctx=len_ctx)
            new_k_layers.append(nk)
            new_v_layers.append(nv)
            tab = _nbr_bias_tiles(na_rpb[j])
            a_parts = [
                _nbr_attn(qkv, cache_k, cache_v, tab, j, n_batch=n_lat_batch, len_latent=len_latent),
                _ctx_attn(qkv, row_block0=n_latent_rows // len_ctx, n_batch=n_ctx_batch, len_ctx=len_ctx),
            ]
            w_o, b_o = na_w_out[j], na_b_out[j]
        x, tb, route, counts = _out_proj_route(a_parts, x_parts, w_o, b_o, mod, i, ln_g[i, 0], ln_b[i, 0],
                                               alpha, moe_w_group[i], moe_b_group[i], moe_w_expert[i],
                                               moe_b_expert[i])
        split = n_latent_rows if i == depth - 1 else None
        x = _moe_layer(x, tb, route, counts, mod, i, moe_w_up, moe_w_down, ln_g[i, 1], ln_b[i, 1], alpha,
                       split_rows=split)
        x_parts = list(x) if split is not None else [x]

    y_sample = x_parts[0].reshape(n_lat_batch, len_latent, d)
    y_prompt = x_parts[1].reshape(n_ctx_batch, len_ctx, d)
    new_k = jnp.concatenate(new_k_layers, axis=1)
    new_v = jnp.concatenate(new_v_layers, axis=1)
    return (y_prompt, y_sample, new_k, new_v)
```
